```python
import jax, jax.numpy as jnp
from jax import lax
import numpy as np

D_MODEL = 2048
BATCH = 1
SEQ = 16384
DEPTH = 1

CHUNK = 64
LEFT_CHUNKS = 8
BAND = (LEFT_CHUNKS + 1) * CHUNK
ATTN_WIDTH = D_MODEL // 2
HEAD_DIM = 64
ATTN_HEADS = ATTN_WIDTH // HEAD_DIM
MAX_REL = 256
REL_TABLE = MAX_REL + CHUNK
POOL_WINDOWS = (2, 4, 8, 16)
POOL_GROUPS = len(POOL_WINDOWS)
POOL_WIDTH = D_MODEL // 2
POOL_GROUP_DIM = POOL_WIDTH // POOL_GROUPS
N_BRANCHES = 2
IN_WIDTH = 3 * ATTN_WIDTH + POOL_WIDTH + N_BRANCHES * D_MODEL
N_EXPERTS = 32
TOP_K = 4
D_FF = D_MODEL
SWIGLU_LIMIT = 7.0
SWIGLU_ALPHA = 1.702
EXPERT_BLOCK = 128
RMS_EPS = 1e-5
NEG_INF = -1e30

kernel_name = "hybrid_chunk_attn_pool_moe_encoder"


def rmsnorm(x, g):
    xf = x.astype(jnp.float32)
    var = jnp.mean(xf * xf, axis=-1, keepdims=True)
    return (xf * lax.rsqrt(var + RMS_EPS) * g.astype(jnp.float32)).astype(x.dtype)


def chunk_attention(q, k, v, rel_bias):
    B, S, H, Dh = q.shape
    n_chunks = S // CHUNK
    pad = ((0, 0), (LEFT_CHUNKS * CHUNK, 0), (0, 0), (0, 0))
    kp = jnp.pad(k, pad)
    vp = jnp.pad(v, pad)
    qc = q.reshape(B, n_chunks, CHUNK, H, Dh).transpose(1, 0, 2, 3, 4)
    dist = LEFT_CHUNKS * CHUNK + jnp.arange(CHUNK)[:, None] - jnp.arange(BAND)[None, :]
    idx = jnp.clip(dist, -(CHUNK - 1), MAX_REL) + (CHUNK - 1)
    bias = rel_bias.astype(jnp.float32)[:, idx]
    scale = HEAD_DIM ** -0.5

    def one_chunk(args):
        c, qb = args
        start = c * CHUNK
        kb = lax.dynamic_slice_in_dim(kp, start, BAND, axis=1)
        vb = lax.dynamic_slice_in_dim(vp, start, BAND, axis=1)
        s = jnp.einsum('bqhd,bkhd->bhqk', qb, kb).astype(jnp.float32) * scale + bias[None]
        key_pos = start + jnp.arange(BAND) - LEFT_CHUNKS * CHUNK
        s = jnp.where((key_pos >= 0)[None, None, None, :], s, NEG_INF)
        p = jax.nn.softmax(s, axis=-1).astype(vb.dtype)
        return jnp.einsum('bhqk,bkhd->bqhd', p, vb)

    o = lax.map(one_chunk, (jnp.arange(n_chunks), qc))
    return o.transpose(1, 0, 2, 3, 4).reshape(B, S, H * Dh)


def multiscale_pool(u, pool_w, pool_scale):
    B, S, _ = u.shape
    uf = u.astype(jnp.float32).reshape(B, S, POOL_GROUPS, POOL_GROUP_DIM)
    cs = jnp.cumsum(uf, axis=1)
    t = jnp.arange(S)
    outs = []
    for gi, w in enumerate(POOL_WINDOWS):
        csg = cs[:, :, gi]
        prev = jnp.pad(csg, ((0, 0), (w, 0), (0, 0)))[:, :S]
        cnt = jnp.minimum(t + 1, w).astype(jnp.float32)[None, :, None]
        outs.append((csg - prev) / cnt - uf[:, :, gi])
    pooled = jnp.stack(outs, axis=2).astype(u.dtype)
    mixed = jnp.einsum('bsgc,gcd->bsgd', pooled, pool_w).reshape(B, S, POOL_WIDTH)
    return mixed * pool_scale


def moe_ffn(h, router_w, router_b, w1, b1, w2, b2):
    B, S, D = h.shape
    n_tok = B * S
    xt = h.reshape(n_tok, D)
    logits = (xt @ router_w).astype(jnp.float32) + router_b.astype(jnp.float32)
    top_val, top_idx = lax.top_k(logits, TOP_K)
    gate = jax.nn.softmax(top_val, axis=-1)
    flat_e = top_idx.reshape(-1)
    flat_tok = jnp.repeat(jnp.arange(n_tok, dtype=jnp.int32), TOP_K)
    flat_w = gate.reshape(-1)
    order = jnp.argsort(flat_e)
    e_sorted = flat_e[order]
    counts = jnp.bincount(flat_e, length=N_EXPERTS)
    start = jnp.cumsum(counts) - counts
    padded = ((counts + EXPERT_BLOCK - 1) // EXPERT_BLOCK) * EXPERT_BLOCK
    pad_end = jnp.cumsum(padded)
    pad_start = pad_end - padded
    dest = pad_start[e_sorted] + (jnp.arange(n_tok * TOP_K) - start[e_sorted])
    n_blocks = -(-(n_tok * TOP_K) // EXPERT_BLOCK) + N_EXPERTS
    n_rows = n_blocks * EXPERT_BLOCK
    row_tok = jnp.full((n_rows,), n_tok, jnp.int32).at[dest].set(flat_tok[order])
    row_w = jnp.zeros((n_rows,), xt.dtype).at[dest].set(flat_w[order].astype(xt.dtype))
    block_e = jnp.minimum(
        jnp.searchsorted(pad_end, jnp.arange(n_blocks) * EXPERT_BLOCK, side='right'), N_EXPERTS - 1)
    x_pad = jnp.concatenate([xt, jnp.zeros((1, D), xt.dtype)], axis=0)

    def expert_block(args):
        tok, e = args
        xb = x_pad[tok]
        hb = xb @ w1[e] + b1[e]
        g = jnp.minimum(hb[:, ::2], SWIGLU_LIMIT)
        u = jnp.clip(hb[:, 1::2], -SWIGLU_LIMIT, SWIGLU_LIMIT)
        a = g * jax.nn.sigmoid(SWIGLU_ALPHA * g) * (u + 1.0)
        return a @ w2[e] + b2[e]

    y = lax.map(expert_block, (row_tok.reshape(n_blocks, EXPERT_BLOCK), block_e))
    y = y.reshape(n_rows, D) * row_w[:, None]
    out = jnp.zeros((n_tok + 1, D), y.dtype).at[row_tok].add(y)[:n_tok]
    return out.reshape(B, S, D)


def setup_inputs(seed: int = 0) -> dict:
    key = jax.random.key(seed)
    ks = jax.random.split(key, 20)
    f32 = jnp.float32
    L = DEPTH
    nrm = lambda k, shape, s: jax.random.normal(k, shape, f32) * s
    return {
        "x": jax.random.normal(ks[0], (BATCH, SEQ, D_MODEL), f32),
        "norm_mix": 1.0 + nrm(ks[1], (L, D_MODEL), 0.05),
        "w_in": nrm(ks[2], (L, D_MODEL, IN_WIDTH), D_MODEL ** -0.5),
        "b_gate": nrm(ks[3], (L, N_BRANCHES * D_MODEL), 0.02),
        "rel_bias": nrm(ks[4], (L, ATTN_HEADS, REL_TABLE), 0.5),
        "w_attn_proj": nrm(ks[5], (L, ATTN_WIDTH, D_MODEL), ATTN_WIDTH ** -0.5),
        "pool_w": nrm(ks[6], (L, POOL_GROUPS, POOL_GROUP_DIM, POOL_GROUP_DIM), POOL_GROUP_DIM ** -0.5),
        "pool_scale": 1.0 + nrm(ks[7], (L, POOL_WIDTH), 0.1),
        "w_pool_proj": nrm(ks[8], (L, POOL_WIDTH, D_MODEL), POOL_WIDTH ** -0.5),
        "w_out": nrm(ks[9], (L, D_MODEL, D_MODEL), D_MODEL ** -0.5),
        "norm_ffn": 1.0 + nrm(ks[10], (L, D_MODEL), 0.05),
        "router_w": nrm(ks[11], (L, D_MODEL, N_EXPERTS), D_MODEL ** -0.5),
        "router_b": nrm(ks[12], (L, N_EXPERTS), 0.01),
        "w1": nrm(ks[13], (L, N_EXPERTS, D_MODEL, 2 * D_FF), D_MODEL ** -0.5),
        "b1": nrm(ks[14], (L, N_EXPERTS, 2 * D_FF), 0.02),
        "w2": nrm(ks[15], (L, N_EXPERTS, D_FF, D_MODEL), D_FF ** -0.5),
        "b2": nrm(ks[16], (L, N_EXPERTS, D_MODEL), 0.02),
        "norm_final": 1.0 + nrm(ks[17], (D_MODEL,), 0.05),
    }


def reference(x, norm_mix, w_in, b_gate, rel_bias, w_attn_proj, pool_w, pool_scale,
              w_pool_proj, w_out, norm_ffn, router_w, router_b, w1, b1, w2, b2, norm_final):
    B, S, D = x.shape
    for l in range(DEPTH):
        xn = rmsnorm(x, norm_mix[l])
        proj = xn @ w_in[l]
        q, k, v, p_in, gate_logits = jnp.split(
            proj, [ATTN_WIDTH, 2 * ATTN_WIDTH, 3 * ATTN_WIDTH, 3 * ATTN_WIDTH + POOL_WIDTH], axis=-1)
        heads = lambda t: t.reshape(B, S, ATTN_HEADS, HEAD_DIM)
        y_attn = chunk_attention(heads(q), heads(k), heads(v), rel_bias[l]) @ w_attn_proj[l]
        y_pool = multiscale_pool(p_in, pool_w[l], pool_scale[l]) @ w_pool_proj[l]
        gates = jax.nn.sigmoid(gate_logits + b_gate[l]).reshape(B, S, N_BRANCHES, D)
        merged = gates[:, :, 0] * y_attn + gates[:, :, 1] * y_pool
        x = x + merged @ w_out[l]
        x = x + moe_ffn(rmsnorm(x, norm_ffn[l]), router_w[l], router_b[l], w1[l], b1[l], w2[l], b2[l])
    return rmsnorm(x, norm_final)
```

```python
import functools

import jax
import jax.numpy as jnp
from jax import lax
from jax.experimental import pallas as pl
from jax.experimental.pallas import tpu as pltpu

D_MODEL = 2048
CHUNK = 64
LEFT_CHUNKS = 8
BAND = (LEFT_CHUNKS + 1) * CHUNK
ATTN_WIDTH = D_MODEL // 2
HEAD_DIM = 64
ATTN_HEADS = ATTN_WIDTH // HEAD_DIM
MAX_REL = 256
POOL_WINDOWS = (2, 4, 8, 16)
POOL_GROUPS = len(POOL_WINDOWS)
POOL_WIDTH = D_MODEL // 2
POOL_GROUP_DIM = POOL_WIDTH // POOL_GROUPS
N_BRANCHES = 2
IN_WIDTH = 3 * ATTN_WIDTH + POOL_WIDTH + N_BRANCHES * D_MODEL
N_EXPERTS = 32
TOP_K = 4
D_FF = D_MODEL
SWIGLU_LIMIT = 7.0
SWIGLU_ALPHA = 1.702
EXPERT_BLOCK = 128
RMS_EPS = 1e-5
NEG_INF = -1e30

LANES = 128
MAX_HALO = max(POOL_WINDOWS)

TM_IN, TN_IN = 1024, 1024
Q_GROUP = 4
TQ = Q_GROUP * CHUNK
KV_BLOCKS = LEFT_CHUNKS // Q_GROUP + 1
KV_SPAN = KV_BLOCKS * TQ
TM_MIX = 512
TM_POST = 512
TM_DISP = 1024
DRAIN_ROWS = 512
TM_COMB = 256

MIB = 1024 * 1024
F32 = jnp.float32
BF16 = jnp.bfloat16


def _cparams(sem, vmem_mib):
    return pltpu.CompilerParams(dimension_semantics=sem, vmem_limit_bytes=vmem_mib * MIB)


def _resident(shape, index_map):
    return pl.BlockSpec(shape, index_map, pipeline_mode=pl.Buffered(1))


def _inproj_kernel(x_ref, g_ref, w_ref, o_ref, xn_ref):
    @pl.when(pl.program_id(1) == 0)
    def _():
        x = x_ref[...]
        var = jnp.mean(x * x, axis=-1, keepdims=True)
        xn_ref[...] = (x * lax.rsqrt(var + RMS_EPS) * g_ref[...]).astype(BF16)

    o_ref[...] = jnp.dot(xn_ref[...], w_ref[...], preferred_element_type=F32).astype(o_ref.dtype)


def _inproj(x2, g, w_bf):
    s = x2.shape[0]
    return pl.pallas_call(
        _inproj_kernel,
        grid=(s // TM_IN, IN_WIDTH // TN_IN),
        in_specs=[
            pl.BlockSpec((TM_IN, D_MODEL), lambda i, j: (i, 0)),
            pl.BlockSpec((1, D_MODEL), lambda i, j: (0, 0)),
            pl.BlockSpec((D_MODEL, TN_IN), lambda i, j: (0, j)),
        ],
        out_specs=pl.BlockSpec((TM_IN, TN_IN), lambda i, j: (i, j)),
        out_shape=jax.ShapeDtypeStruct((s, IN_WIDTH), BF16),
        scratch_shapes=[pltpu.VMEM((TM_IN, D_MODEL), BF16)],
        compiler_params=_cparams(("parallel", "arbitrary"), 48),
        name="inproj",
    )(x2, g, w_bf)


def _attn_kernel(q_ref, k0_ref, k1_ref, k2_ref, v0_ref, v1_ref, v2_ref, bias_ref, o_ref):
    i = pl.program_id(0)
    k_refs = (k0_ref, k1_ref, k2_ref)
    v_refs = (v0_ref, v1_ref, v2_ref)
    col = lax.broadcasted_iota(jnp.int32, (1, KV_SPAN), 1)
    valid = col >= (LEFT_CHUNKS * CHUNK - TQ * i)
    lane = lax.broadcasted_iota(jnp.int32, (1, LANES), 1)
    scale = HEAD_DIM ** -0.5
    nt = (((1,), (1,)), ((), ()))
    for hp in range(ATTN_HEADS // 2):
        cs = slice(hp * LANES, (hp + 1) * LANES)
        qp = q_ref[:, cs]
        ks = [r[:, cs] for r in k_refs]
        vs = [r[:, cs] for r in v_refs]
        outs = []
        for hh in range(2):
            head_lanes = (lane >= hh * HEAD_DIM) & (lane < (hh + 1) * HEAD_DIM)
            qh = jnp.where(head_lanes, qp, jnp.zeros_like(qp))
            s = jnp.concatenate(
                [lax.dot_general(qh, kb, nt, preferred_element_type=F32) for kb in ks], axis=1)
            s = s * scale + bias_ref[2 * hp + hh]
            s = jnp.where(valid, s, NEG_INF)
            m = jnp.max(s, axis=-1, keepdims=True)
            p = jnp.exp(s - m)
            l = jnp.sum(p, axis=-1, keepdims=True)
            pb = p.astype(BF16)
            o = jnp.dot(pb[:, 0:TQ], vs[0], preferred_element_type=F32)
            o += jnp.dot(pb[:, TQ:2 * TQ], vs[1], preferred_element_type=F32)
            o += jnp.dot(pb[:, 2 * TQ:3 * TQ], vs[2], preferred_element_type=F32)
            outs.append(o / l)
        o_ref[:, cs] = jnp.where(lane < HEAD_DIM, outs[0], outs[1]).astype(o_ref.dtype)


def _attn_bias_table(rel_bias):
    dist = LEFT_CHUNKS * CHUNK + jnp.arange(CHUNK)[:, None] - jnp.arange(BAND)[None, :]
    idx = jnp.clip(dist, -(CHUNK - 1), MAX_REL) + (CHUNK - 1)
    band_bias = rel_bias.astype(F32)[:, idx]
    rows = [
        jnp.pad(band_bias, ((0, 0), (0, 0), (c * CHUNK, KV_SPAN - BAND - c * CHUNK)),
                constant_values=NEG_INF)
        for c in range(Q_GROUP)
    ]
    return jnp.concatenate(rows, axis=1)


def _attention(proj, bias_tab):
    s = proj.shape[0]
    kcol, vcol = 1, 2

    def kv_spec(back, colblk):
        return pl.BlockSpec((TQ, ATTN_WIDTH), lambda i: (jnp.maximum(i - back, 0), colblk))

    return pl.pallas_call(
        _attn_kernel,
        grid=(s // TQ,),
        in_specs=[
            pl.BlockSpec((TQ, ATTN_WIDTH), lambda i: (i, 0)),
            kv_spec(2, kcol), kv_spec(1, kcol), kv_spec(0, kcol),
            kv_spec(2, vcol), kv_spec(1, vcol), kv_spec(0, vcol),
            _resident((ATTN_HEADS, TQ, KV_SPAN), lambda i: (0, 0, 0)),
        ],
        out_specs=pl.BlockSpec((TQ, ATTN_WIDTH), lambda i: (i, 0)),
        out_shape=jax.ShapeDtypeStruct((s, ATTN_WIDTH), BF16),
        compiler_params=_cparams(("parallel",), 48),
        name="attn",
    )(proj, proj, proj, proj, proj, proj, proj, bias_tab)


def _mix_kernel(attn_ref, pin_ref, gl_ref, bg_ref, wap_ref, wpp_ref, pw_ref, ps_ref,
                o_ref, halo_ref, win_ref, hwin_ref):
    i = pl.program_id(0)
    tm = TM_MIX

    @pl.when(i == 0)
    def _():
        halo_ref[...] = jnp.zeros_like(halo_ref)
        t = lax.broadcasted_iota(jnp.int32, (tm, tm), 0)
        j = lax.broadcasted_iota(jnp.int32, (tm, tm), 1)
        th = lax.broadcasted_iota(jnp.int32, (tm, MAX_HALO), 0)
        jh = lax.broadcasted_iota(jnp.int32, (tm, MAX_HALO), 1)
        for gi, w in enumerate(POOL_WINDOWS):
            win_ref[gi] = jnp.where((t - j >= 0) & (t - j < w), 1.0, 0.0).astype(BF16)
            hwin_ref[gi] = jnp.where(th + MAX_HALO - jh < w, 1.0, 0.0).astype(BF16)

    pos = i * tm + lax.broadcasted_iota(jnp.int32, (tm, 1), 0)
    u_all = pin_ref[...]
    halo = halo_ref[...]
    mixed = []
    for gi, w in enumerate(POOL_WINDOWS):
        cs = slice(gi * POOL_GROUP_DIM, (gi + 1) * POOL_GROUP_DIM)
        u = u_all[:, cs]
        wsum = jnp.dot(win_ref[gi], u, preferred_element_type=F32)
        wsum += jnp.dot(hwin_ref[gi], halo[:, cs], preferred_element_type=F32)
        inv_cnt = jnp.where(pos + 1 >= w, 1.0 / w, 1.0 / jnp.minimum(pos + 1, w).astype(F32))
        pooled = wsum * inv_cnt - u.astype(F32)
        mixed.append(jnp.dot(pooled.astype(BF16), pw_ref[gi], preferred_element_type=F32))
    mixed = jnp.concatenate(mixed, axis=1) * ps_ref[...]
    halo_ref[...] = u_all[tm - MAX_HALO:, :]

    y_pool = jnp.dot(mixed.astype(BF16), wpp_ref[...], preferred_element_type=F32)
    y_attn = jnp.dot(attn_ref[...], wap_ref[...], preferred_element_type=F32)
    gates = jax.nn.sigmoid(gl_ref[...].astype(F32) + bg_ref[...])
    merged = gates[:, :D_MODEL] * y_attn + gates[:, D_MODEL:] * y_pool
    o_ref[...] = merged.astype(o_ref.dtype)


def _mix(attn, proj, b_gate, wap_bf, wpp_bf, pw_bf, pool_scale):
    s = attn.shape[0]
    tm = TM_MIX
    pin_col = 3 * ATTN_WIDTH // POOL_WIDTH
    gl_col = (3 * ATTN_WIDTH + POOL_WIDTH) // (N_BRANCHES * D_MODEL)
    return pl.pallas_call(
        _mix_kernel,
        grid=(s // tm,),
        in_specs=[
            pl.BlockSpec((tm, ATTN_WIDTH), lambda i: (i, 0)),
            pl.BlockSpec((tm, POOL_WIDTH), lambda i: (i, pin_col)),
            pl.BlockSpec((tm, N_BRANCHES * D_MODEL), lambda i: (i, gl_col)),
            pl.BlockSpec((1, N_BRANCHES * D_MODEL), lambda i: (0, 0)),
            _resident((ATTN_WIDTH, D_MODEL), lambda i: (0, 0)),
            _resident((POOL_WIDTH, D_MODEL), lambda i: (0, 0)),
            _resident((POOL_GROUPS, POOL_GROUP_DIM, POOL_GROUP_DIM), lambda i: (0, 0, 0)),
            pl.BlockSpec((1, POOL_WIDTH), lambda i: (0, 0)),
        ],
        out_specs=pl.BlockSpec((tm, D_MODEL), lambda i: (i, 0)),
        out_shape=jax.ShapeDtypeStruct((s, D_MODEL), BF16),
        scratch_shapes=[
            pltpu.VMEM((MAX_HALO, POOL_WIDTH), BF16),
            pltpu.VMEM((POOL_GROUPS, tm, tm), BF16),
            pltpu.VMEM((POOL_GROUPS, tm, MAX_HALO), BF16),
        ],
        compiler_params=_cparams(("arbitrary",), 48),
        name="mix",
    )(attn, proj, proj, b_gate, wap_bf, wpp_bf, pw_bf, pool_scale)


META_E, META_POS, META_W = 0, TOP_K, 2 * TOP_K


def _post_kernel(x_ref, m_ref, wout_ref, g_ref, rwh_ref, rwl_ref, rb_ref,
                 x1_ref, h_ref, meta_ref, cnt_ref, carry_ref, tri_ref):
    i = pl.program_id(0)
    tm = TM_POST

    @pl.when(i == 0)
    def _():
        carry_ref[...] = jnp.zeros_like(carry_ref)
        t = lax.broadcasted_iota(jnp.int32, (tm, tm), 0)
        j = lax.broadcasted_iota(jnp.int32, (tm, tm), 1)
        tri_ref[...] = jnp.where(j < t, 1.0, 0.0).astype(BF16)

    x1 = x_ref[...] + jnp.dot(m_ref[...], wout_ref[...], preferred_element_type=F32)
    x1_ref[...] = x1
    var = jnp.mean(x1 * x1, axis=-1, keepdims=True)
    h = x1 * lax.rsqrt(var + RMS_EPS) * g_ref[...]
    h_ref[...] = h

    h_hi = h.astype(BF16)
    h_lo = (h - h_hi.astype(F32)).astype(BF16)
    logits = (jnp.dot(h_hi, rwh_ref[...], preferred_element_type=F32)
              + jnp.dot(h_hi, rwl_ref[...], preferred_element_type=F32)
              + jnp.dot(h_lo, rwh_ref[...], preferred_element_type=F32)
              + rb_ref[...])

    lane_e = lax.broadcasted_iota(jnp.int32, (tm, N_EXPERTS), 1).astype(F32)
    work = logits
    sel = jnp.zeros((tm, N_EXPERTS), F32)
    picks = []
    for _ in range(TOP_K):
        m = jnp.max(work, axis=-1, keepdims=True)
        idx = jnp.min(jnp.where(work == m, lane_e, float(N_EXPERTS)), axis=-1, keepdims=True)
        onehot = lane_e == idx
        picks.append((m, idx, onehot))
        sel = jnp.where(onehot, 1.0, sel)
        work = jnp.where(onehot, -jnp.inf, work)

    rank = jnp.dot(tri_ref[...], sel.astype(BF16), preferred_element_type=F32) + carry_ref[...]
    carry_ref[...] += jnp.sum(sel, axis=0, keepdims=True)
    cnt_ref[...] = carry_ref[...]

    top = picks[0][0]
    exps = [jnp.exp(m - top) for m, _, _ in picks]
    denom = exps[0] + exps[1] + exps[2] + exps[3]
    lane = lax.broadcasted_iota(jnp.int32, (tm, LANES), 1)
    meta = jnp.zeros((tm, LANES), F32)
    for k, (m, idx, onehot) in enumerate(picks):
        pos_k = jnp.sum(jnp.where(onehot, rank, 0.0), axis=-1, keepdims=True)
        meta = jnp.where(lane == META_E + k, idx, meta)
        meta = jnp.where(lane == META_POS + k, pos_k, meta)
        meta = jnp.where(lane == META_W + k, exps[k] / denom, meta)
    meta_ref[...] = meta


def _post(x2, merged, wout_bf, g, rw_hi, rw_lo, rb):
    s = x2.shape[0]
    tm = TM_POST
    return pl.pallas_call(
        _post_kernel,
        grid=(s // tm,),
        in_specs=[
            pl.BlockSpec((tm, D_MODEL), lambda i: (i, 0)),
            pl.BlockSpec((tm, D_MODEL), lambda i: (i, 0)),
            _resident((D_MODEL, D_MODEL), lambda i: (0, 0)),
            pl.BlockSpec((1, D_MODEL), lambda i: (0, 0)),
            pl.BlockSpec((D_MODEL, N_EXPERTS), lambda i: (0, 0)),
            pl.BlockSpec((D_MODEL, N_EXPERTS), lambda i: (0, 0)),
            pl.BlockSpec((1, N_EXPERTS), lambda i: (0, 0)),
        ],
        out_specs=[
            pl.BlockSpec((tm, D_MODEL), lambda i: (i, 0)),
            pl.BlockSpec((tm, D_MODEL), lambda i: (i, 0)),
            pl.BlockSpec((tm, LANES), lambda i: (i, 0)),
            pl.BlockSpec((1, N_EXPERTS), lambda i: (0, 0)),
        ],
        out_shape=[
            jax.ShapeDtypeStruct((s, D_MODEL), F32),
            jax.ShapeDtypeStruct((s, D_MODEL), F32),
            jax.ShapeDtypeStruct((s, LANES), F32),
            jax.ShapeDtypeStruct((1, N_EXPERTS), F32),
        ],
        scratch_shapes=[
            pltpu.VMEM((1, N_EXPERTS), F32),
            pltpu.VMEM((tm, tm), BF16),
        ],
        compiler_params=_cparams(("arbitrary",), 48),
        name="post",
    )(x2, merged, wout_bf, g, rw_hi, rw_lo, rb)


def _row_copy(src, src_row, dst, dst_row, sem):
    return pltpu.make_async_copy(src.at[pl.ds(src_row, 1)], dst.at[pl.ds(dst_row, 1)], sem)


def _block_copy(src, dst, dst_block, sem):
    return pltpu.make_async_copy(src, dst.at[pl.ds(dst_block * EXPERT_BLOCK, EXPERT_BLOCK)], sem)


def _dispatch_kernel(padbase_ref, padcnt_ref, nreal_ref, dest_ref, h_hbm, zero_hbm, xs_hbm,
                     sem, pad_sem):
    i = pl.program_id(0)
    tm = TM_DISP
    n_blocks = xs_hbm.shape[0] // EXPERT_BLOCK

    def issue(t, c):
        for k in range(TOP_K):
            _row_copy(h_hbm, i * tm + t, xs_hbm, dest_ref[0, 0, TOP_K * t + k], sem).start()
        return c

    lax.fori_loop(0, tm, issue, 0, unroll=8)

    @pl.when(i == 0)
    def _():
        for e in range(N_EXPERTS):
            def fill(r, c, e=e):
                _row_copy(zero_hbm, 0, xs_hbm, padbase_ref[e] + r, pad_sem).start()
                return c
            lax.fori_loop(0, padcnt_ref[e], fill, 0)

        def fill_block(b, c):
            _block_copy(zero_hbm, xs_hbm, b, pad_sem).start()
            return c
        lax.fori_loop(nreal_ref[0], n_blocks, fill_block, 0)

        for e in range(N_EXPERTS):
            def drain(r, c):
                _row_copy(zero_hbm, 0, xs_hbm, 0, pad_sem).wait()
                return c
            lax.fori_loop(0, padcnt_ref[e], drain, 0)

        def drain_block(b, c):
            _block_copy(zero_hbm, xs_hbm, 0, pad_sem).wait()
            return c
        lax.fori_loop(nreal_ref[0], n_blocks, drain_block, 0)

    for _ in range(TOP_K * tm // DRAIN_ROWS):
        pltpu.make_async_copy(h_hbm.at[pl.ds(0, DRAIN_ROWS)], xs_hbm.at[pl.ds(0, DRAIN_ROWS)],
                              sem).wait()


def _dispatch(padbase, padcnt, nreal, dest, h, n_rows):
    s = h.shape[0]
    tm = TM_DISP
    dest3 = dest.reshape(s // tm, 1, TOP_K * tm)
    zero_rows = jnp.zeros((EXPERT_BLOCK, D_MODEL), F32)
    grid_spec = pltpu.PrefetchScalarGridSpec(
        num_scalar_prefetch=3,
        grid=(s // tm,),
        in_specs=[
            pl.BlockSpec((1, 1, TOP_K * tm), lambda i, pb, pc, nr: (i, 0, 0),
                         memory_space=pltpu.SMEM),
            pl.BlockSpec(memory_space=pl.ANY),
            pl.BlockSpec(memory_space=pl.ANY),
        ],
        out_specs=pl.BlockSpec(memory_space=pl.ANY),
        scratch_shapes=[pltpu.SemaphoreType.DMA, pltpu.SemaphoreType.DMA],
    )
    return pl.pallas_call(
        _dispatch_kernel,
        grid_spec=grid_spec,
        out_shape=jax.ShapeDtypeStruct((n_rows, D_MODEL), F32),
        compiler_params=_cparams(("arbitrary",), 32),
        name="dispatch",
    )(padbase, padcnt, nreal, dest3, h, zero_rows)


def _up_kernel(be_ref, nreal_ref, x_ref, wg_ref, wu_ref, bg_ref, bu_ref, a_ref):
    b = pl.program_id(0)

    @pl.when(b < nreal_ref[0])
    def _():
        x = x_ref[...].astype(BF16)
        g = jnp.dot(x, wg_ref[0], preferred_element_type=F32) + bg_ref[0]
        u = jnp.dot(x, wu_ref[0], preferred_element_type=F32) + bu_ref[0]
        g = jnp.minimum(g, SWIGLU_LIMIT)
        u = jnp.clip(u, -SWIGLU_LIMIT, SWIGLU_LIMIT)
        a_ref[...] = (g * jax.nn.sigmoid(SWIGLU_ALPHA * g) * (u + 1.0)).astype(a_ref.dtype)

    @pl.when(b >= nreal_ref[0])
    def _():
        a_ref[...] = jnp.zeros_like(a_ref)


def _down_kernel(be_ref, nreal_ref, a_ref, w_ref, b_ref, y_ref):
    b = pl.program_id(0)

    @pl.when(b < nreal_ref[0])
    def _():
        y_ref[...] = jnp.dot(a_ref[...], w_ref[0], preferred_element_type=F32) + b_ref[0]

    @pl.when(b >= nreal_ref[0])
    def _():
        y_ref[...] = jnp.zeros_like(y_ref)


def _row_block(b, be, nr):
    return (jnp.minimum(b, nr[0] - 1), 0)


def _expert_block(b, be, nr):
    return (be[b], 0, 0)


def _experts(block_e, nreal, xs, w1g, w1u, b1g, b1u, w2_bf, b2):
    n_rows = xs.shape[0]
    nb = n_rows // EXPERT_BLOCK
    act = pl.pallas_call(
        _up_kernel,
        grid_spec=pltpu.PrefetchScalarGridSpec(
            num_scalar_prefetch=2,
            grid=(nb,),
            in_specs=[
                pl.BlockSpec((EXPERT_BLOCK, D_MODEL), _row_block),
                pl.BlockSpec((1, D_MODEL, D_FF), _expert_block),
                pl.BlockSpec((1, D_MODEL, D_FF), _expert_block),
                pl.BlockSpec((1, 1, D_FF), _expert_block),
                pl.BlockSpec((1, 1, D_FF), _expert_block),
            ],
            out_specs=pl.BlockSpec((EXPERT_BLOCK, D_FF), lambda b, be, nr: (b, 0)),
        ),
        out_shape=jax.ShapeDtypeStruct((n_rows, D_FF), BF16),
        compiler_params=_cparams(("arbitrary",), 56),
        name="expert_up",
    )(block_e, nreal, xs, w1g, w1u, b1g, b1u)
    return pl.pallas_call(
        _down_kernel,
        grid_spec=pltpu.PrefetchScalarGridSpec(
            num_scalar_prefetch=2,
            grid=(nb,),
            in_specs=[
                pl.BlockSpec((EXPERT_BLOCK, D_FF), _row_block),
                pl.BlockSpec((1, D_FF, D_MODEL), _expert_block),
                pl.BlockSpec((1, 1, D_MODEL), _expert_block),
            ],
            out_specs=pl.BlockSpec((EXPERT_BLOCK, D_MODEL), lambda b, be, nr: (b, 0)),
        ),
        out_shape=jax.ShapeDtypeStruct((n_rows, D_MODEL), F32),
        compiler_params=_cparams(("arbitrary",), 40),
        name="expert_down",
    )(block_e, nreal, act, w2_bf, b2)


def _combine_kernel(dcur_ref, dnxt_ref, x1_ref, meta_ref, g_ref, y_hbm, o_ref, ybuf, sem):
    i = pl.program_id(0)
    n = pl.num_programs(0)
    tm = TM_COMB
    slot = i % 2

    def issue(dref, s):
        def body(t, c):
            for k in range(TOP_K):
                pltpu.make_async_copy(y_hbm.at[pl.ds(dref[0, 0, TOP_K * t + k], 1)],
                                      ybuf.at[s, k, pl.ds(t, 1)], sem.at[s]).start()
            return c
        lax.fori_loop(0, tm, body, 0, unroll=8)

    @pl.when(i == 0)
    def _():
        issue(dcur_ref, 0)

    @pl.when(i + 1 < n)
    def _():
        issue(dnxt_ref, 1 - slot)

    for k in range(TOP_K):
        pltpu.make_async_copy(y_hbm.at[pl.ds(0, tm)], ybuf.at[slot, k], sem.at[slot]).wait()

    meta = meta_ref[...]
    acc = x1_ref[...]
    for k in range(TOP_K):
        acc = acc + meta[:, META_W + k:META_W + k + 1] * ybuf[slot, k]
    var = jnp.mean(acc * acc, axis=-1, keepdims=True)
    o_ref[...] = (acc * lax.rsqrt(var + RMS_EPS) * g_ref[...]).astype(o_ref.dtype)


def _combine(dest, x1, meta, g, y):
    s = x1.shape[0]
    tm = TM_COMB
    nt = s // tm
    dest3 = dest.reshape(nt, 1, TOP_K * tm)
    return pl.pallas_call(
        _combine_kernel,
        grid=(nt,),
        in_specs=[
            pl.BlockSpec((1, 1, TOP_K * tm), lambda i: (i, 0, 0), memory_space=pltpu.SMEM),
            pl.BlockSpec((1, 1, TOP_K * tm), lambda i: (jnp.minimum(i + 1, nt - 1), 0, 0),
                         memory_space=pltpu.SMEM),
            pl.BlockSpec((tm, D_MODEL), lambda i: (i, 0)),
            pl.BlockSpec((tm, LANES), lambda i: (i, 0)),
            pl.BlockSpec((1, D_MODEL), lambda i: (0, 0)),
            pl.BlockSpec(memory_space=pl.ANY),
        ],
        out_specs=pl.BlockSpec((tm, D_MODEL), lambda i: (i, 0)),
        out_shape=jax.ShapeDtypeStruct((s, D_MODEL), F32),
        scratch_shapes=[
            pltpu.VMEM((2, TOP_K, tm, D_MODEL), F32),
            pltpu.SemaphoreType.DMA((2,)),
        ],
        compiler_params=_cparams(("arbitrary",), 40),
        name="combine",
    )(dest3, dest3, x1, meta, g, y)


def _routing_tables(meta, cnt, n_blocks):
    counts = cnt[0].astype(jnp.int32)
    padded = ((counts + EXPERT_BLOCK - 1) // EXPERT_BLOCK) * EXPERT_BLOCK
    pad_end = jnp.cumsum(padded)
    pad_start = pad_end - padded
    e4 = meta[:, META_E:META_E + TOP_K].astype(jnp.int32)
    pos4 = meta[:, META_POS:META_POS + TOP_K].astype(jnp.int32)
    dest = pad_start[e4] + pos4
    nreal = pad_end[-1:] // EXPERT_BLOCK
    blk = jnp.minimum(jnp.arange(n_blocks, dtype=jnp.int32), nreal[0] - 1)
    block_e = jnp.searchsorted(pad_end, blk * EXPERT_BLOCK, side="right").astype(jnp.int32)
    return dest, block_e, nreal.astype(jnp.int32), (pad_start + counts).astype(jnp.int32), \
        (padded - counts).astype(jnp.int32)


def kernel(x, norm_mix, w_in, b_gate, rel_bias, w_attn_proj, pool_w, pool_scale, w_pool_proj,
           w_out, norm_ffn, router_w, router_b, w1, b1, w2, b2, norm_final):
    bsz, seq, d = x.shape
    n_tok = bsz * seq
    assert w_in.shape[0] == 1 and bsz == 1 and d == D_MODEL and seq % TM_IN == 0
    n_blocks = -(-(n_tok * TOP_K) // EXPERT_BLOCK) + N_EXPERTS
    n_rows = n_blocks * EXPERT_BLOCK
    x2 = x.reshape(n_tok, d)
    row = lambda v: v.reshape(1, -1).astype(F32)
    l = 0
    proj = _inproj(x2, row(norm_mix[l]), w_in[l].astype(BF16))
    attn = _attention(proj, _attn_bias_table(rel_bias[l]))
    merged = _mix(attn, proj, row(b_gate[l]), w_attn_proj[l].astype(BF16),
                  w_pool_proj[l].astype(BF16), pool_w[l].astype(BF16), row(pool_scale[l]))
    rw = router_w[l].astype(F32)
    rw_hi = rw.astype(BF16)
    rw_lo = (rw - rw_hi.astype(F32)).astype(BF16)
    x1, h, meta, cnt = _post(x2, merged, w_out[l].astype(BF16), row(norm_ffn[l]),
                             rw_hi, rw_lo, row(router_b[l]))
    dest, block_e, nreal, padbase, padcnt = _routing_tables(meta, cnt, n_blocks)
    xs = _dispatch(padbase, padcnt, nreal, dest, h, n_rows)
    y = _experts(block_e, nreal, xs,
                 w1[l][:, :, 0::2].astype(BF16), w1[l][:, :, 1::2].astype(BF16),
                 b1[l][:, None, 0::2].astype(F32), b1[l][:, None, 1::2].astype(F32),
                 w2[l].astype(BF16), b2[l][:, None, :].astype(F32))
    out = _combine(dest, x1, meta, row(norm_final), y)
    return out.reshape(bsz, seq, d)
```

```python
import functools

import jax
import jax.numpy as jnp
from jax import lax
from jax.experimental import pallas as pl
from jax.experimental.pallas import tpu as pltpu

D_MODEL = 2048
CHUNK = 64
LEFT_CHUNKS = 8
BAND = (LEFT_CHUNKS + 1) * CHUNK
ATTN_WIDTH = D_MODEL // 2
HEAD_DIM = 64
ATTN_HEADS = ATTN_WIDTH // HEAD_DIM
MAX_REL = 256
POOL_WINDOWS = (2, 4, 8, 16)
POOL_GROUPS = len(POOL_WINDOWS)
POOL_WIDTH = D_MODEL // 2
POOL_GROUP_DIM = POOL_WIDTH // POOL_GROUPS
N_BRANCHES = 2
IN_WIDTH = 3 * ATTN_WIDTH + POOL_WIDTH + N_BRANCHES * D_MODEL
N_EXPERTS = 32
TOP_K = 4
D_FF = D_MODEL
SWIGLU_LIMIT = 7.0
SWIGLU_ALPHA = 1.702
EXPERT_BLOCK = 128
RMS_EPS = 1e-5
NEG_INF = -1e30

LANES = 128
MAX_HALO = max(POOL_WINDOWS)

TM_IN, TN_IN = 1024, 1024
Q_GROUP = 4
TQ = Q_GROUP * CHUNK
KV_BLOCKS = LEFT_CHUNKS // Q_GROUP + 1
KV_SPAN = KV_BLOCKS * TQ
TM_MIX = 512
TM_POST = 512
TM_DISP = 1024
DRAIN_ROWS = 512
W1_TILE = 512
TM_COMB = 256

MIB = 1024 * 1024
F32 = jnp.float32
BF16 = jnp.bfloat16


def _cparams(sem, vmem_mib):
    return pltpu.CompilerParams(dimension_semantics=sem, vmem_limit_bytes=vmem_mib * MIB)


def _resident(shape, index_map):
    return pl.BlockSpec(shape, index_map, pipeline_mode=pl.Buffered(1))


def _inproj_kernel(x_ref, g_ref, w_ref, o_ref, xn_ref):
    @pl.when(pl.program_id(1) == 0)
    def _():
        x = x_ref[...]
        var = jnp.mean(x * x, axis=-1, keepdims=True)
        xn_ref[...] = (x * lax.rsqrt(var + RMS_EPS) * g_ref[...]).astype(BF16)

    o_ref[...] = jnp.dot(xn_ref[...], w_ref[...], preferred_element_type=F32).astype(o_ref.dtype)


def _inproj(x2, g, w_bf):
    s = x2.shape[0]
    return pl.pallas_call(
        _inproj_kernel,
        grid=(s // TM_IN, IN_WIDTH // TN_IN),
        in_specs=[
            pl.BlockSpec((TM_IN, D_MODEL), lambda i, j: (i, 0)),
            pl.BlockSpec((1, D_MODEL), lambda i, j: (0, 0)),
            pl.BlockSpec((D_MODEL, TN_IN), lambda i, j: (0, j)),
        ],
        out_specs=pl.BlockSpec((TM_IN, TN_IN), lambda i, j: (i, j)),
        out_shape=jax.ShapeDtypeStruct((s, IN_WIDTH), BF16),
        scratch_shapes=[pltpu.VMEM((TM_IN, D_MODEL), BF16)],
        compiler_params=_cparams(("parallel", "arbitrary"), 48),
        name="inproj",
    )(x2, g, w_bf)


def _attn_kernel(q_ref, k0_ref, k1_ref, k2_ref, v0_ref, v1_ref, v2_ref, bias_ref, o_ref):
    i = pl.program_id(0)
    k_refs = (k0_ref, k1_ref, k2_ref)
    v_refs = (v0_ref, v1_ref, v2_ref)
    col = lax.broadcasted_iota(jnp.int32, (1, KV_SPAN), 1)
    valid = col >= (LEFT_CHUNKS * CHUNK - TQ * i)
    lane = lax.broadcasted_iota(jnp.int32, (1, LANES), 1)
    scale = HEAD_DIM ** -0.5
    nt = (((1,), (1,)), ((), ()))
    for hp in range(ATTN_HEADS // 2):
        cs = slice(hp * LANES, (hp + 1) * LANES)
        qp = q_ref[:, cs]
        ks = [r[:, cs] for r in k_refs]
        vs = [r[:, cs] for r in v_refs]
        outs = []
        for hh in range(2):
            head_lanes = (lane >= hh * HEAD_DIM) & (lane < (hh + 1) * HEAD_DIM)
            qh = jnp.where(head_lanes, qp, jnp.zeros_like(qp))
            s = jnp.concatenate(
                [lax.dot_general(qh, kb, nt, preferred_element_type=F32) for kb in ks], axis=1)
            s = s * scale + bias_ref[2 * hp + hh]
            s = jnp.where(valid, s, NEG_INF)
            m = jnp.max(s, axis=-1, keepdims=True)
            p = jnp.exp(s - m)
            l = jnp.sum(p, axis=-1, keepdims=True)
            pb = p.astype(BF16)
            o = jnp.dot(pb[:, 0:TQ], vs[0], preferred_element_type=F32)
            o += jnp.dot(pb[:, TQ:2 * TQ], vs[1], preferred_element_type=F32)
            o += jnp.dot(pb[:, 2 * TQ:3 * TQ], vs[2], preferred_element_type=F32)
            outs.append(o / l)
        o_ref[:, cs] = jnp.where(lane < HEAD_DIM, outs[0], outs[1]).astype(o_ref.dtype)


def _attn_bias_table(rel_bias):
    dist = LEFT_CHUNKS * CHUNK + jnp.arange(CHUNK)[:, None] - jnp.arange(BAND)[None, :]
    idx = jnp.clip(dist, -(CHUNK - 1), MAX_REL) + (CHUNK - 1)
    band_bias = rel_bias.astype(F32)[:, idx]
    rows = [
        jnp.pad(band_bias, ((0, 0), (0, 0), (c * CHUNK, KV_SPAN - BAND - c * CHUNK)),
                constant_values=NEG_INF)
        for c in range(Q_GROUP)
    ]
    return jnp.concatenate(rows, axis=1)


def _attention(proj, bias_tab):
    s = proj.shape[0]
    kcol, vcol = 1, 2

    def kv_spec(back, colblk):
        return pl.BlockSpec((TQ, ATTN_WIDTH), lambda i: (jnp.maximum(i - back, 0), colblk))

    return pl.pallas_call(
        _attn_kernel,
        grid=(s // TQ,),
        in_specs=[
            pl.BlockSpec((TQ, ATTN_WIDTH), lambda i: (i, 0)),
            kv_spec(2, kcol), kv_spec(1, kcol), kv_spec(0, kcol),
            kv_spec(2, vcol), kv_spec(1, vcol), kv_spec(0, vcol),
            _resident((ATTN_HEADS, TQ, KV_SPAN), lambda i: (0, 0, 0)),
        ],
        out_specs=pl.BlockSpec((TQ, ATTN_WIDTH), lambda i: (i, 0)),
        out_shape=jax.ShapeDtypeStruct((s, ATTN_WIDTH), BF16),
        compiler_params=_cparams(("parallel",), 48),
        name="attn",
    )(proj, proj, proj, proj, proj, proj, proj, bias_tab)


def _mix_kernel(attn_ref, pin_ref, gl_ref, bg_ref, wap_ref, wpp_ref, pw_ref, ps_ref,
                o_ref, halo_ref, win_ref, hwin_ref):
    i = pl.program_id(0)
    tm = TM_MIX

    @pl.when(i == 0)
    def _():
        halo_ref[...] = jnp.zeros_like(halo_ref)
        t = lax.broadcasted_iota(jnp.int32, (tm, tm), 0)
        j = lax.broadcasted_iota(jnp.int32, (tm, tm), 1)
        th = lax.broadcasted_iota(jnp.int32, (tm, MAX_HALO), 0)
        jh = lax.broadcasted_iota(jnp.int32, (tm, MAX_HALO), 1)
        for gi, w in enumerate(POOL_WINDOWS):
            win_ref[gi] = jnp.where((t - j >= 0) & (t - j < w), 1.0, 0.0).astype(BF16)
            hwin_ref[gi] = jnp.where(th + MAX_HALO - jh < w, 1.0, 0.0).astype(BF16)

    pos = i * tm + lax.broadcasted_iota(jnp.int32, (tm, 1), 0)
    u_all = pin_ref[...]
    halo = halo_ref[...]
    mixed = []
    for gi, w in enumerate(POOL_WINDOWS):
        cs = slice(gi * POOL_GROUP_DIM, (gi + 1) * POOL_GROUP_DIM)
        u = u_all[:, cs]
        wsum = jnp.dot(win_ref[gi], u, preferred_element_type=F32)
        wsum += jnp.dot(hwin_ref[gi], halo[:, cs], preferred_element_type=F32)
        inv_cnt = jnp.where(pos + 1 >= w, 1.0 / w, 1.0 / jnp.minimum(pos + 1, w).astype(F32))
        pooled = wsum * inv_cnt - u.astype(F32)
        mixed.append(jnp.dot(pooled.astype(BF16), pw_ref[gi], preferred_element_type=F32))
    mixed = jnp.concatenate(mixed, axis=1) * ps_ref[...]
    halo_ref[...] = u_all[tm - MAX_HALO:, :]

    y_pool = jnp.dot(mixed.astype(BF16), wpp_ref[...], preferred_element_type=F32)
    y_attn = jnp.dot(attn_ref[...], wap_ref[...], preferred_element_type=F32)
    gates = jax.nn.sigmoid(gl_ref[...].astype(F32) + bg_ref[...])
    merged = gates[:, :D_MODEL] * y_attn + gates[:, D_MODEL:] * y_pool
    o_ref[...] = merged.astype(o_ref.dtype)


def _mix(attn, proj, b_gate, wap_bf, wpp_bf, pw_bf, pool_scale):
    s = attn.shape[0]
    tm = TM_MIX
    pin_col = 3 * ATTN_WIDTH // POOL_WIDTH
    gl_col = (3 * ATTN_WIDTH + POOL_WIDTH) // (N_BRANCHES * D_MODEL)
    return pl.pallas_call(
        _mix_kernel,
        grid=(s // tm,),
        in_specs=[
            pl.BlockSpec((tm, ATTN_WIDTH), lambda i: (i, 0)),
            pl.BlockSpec((tm, POOL_WIDTH), lambda i: (i, pin_col)),
            pl.BlockSpec((tm, N_BRANCHES * D_MODEL), lambda i: (i, gl_col)),
            pl.BlockSpec((1, N_BRANCHES * D_MODEL), lambda i: (0, 0)),
            _resident((ATTN_WIDTH, D_MODEL), lambda i: (0, 0)),
            _resident((POOL_WIDTH, D_MODEL), lambda i: (0, 0)),
            _resident((POOL_GROUPS, POOL_GROUP_DIM, POOL_GROUP_DIM), lambda i: (0, 0, 0)),
            pl.BlockSpec((1, POOL_WIDTH), lambda i: (0, 0)),
        ],
        out_specs=pl.BlockSpec((tm, D_MODEL), lambda i: (i, 0)),
        out_shape=jax.ShapeDtypeStruct((s, D_MODEL), BF16),
        scratch_shapes=[
            pltpu.VMEM((MAX_HALO, POOL_WIDTH), BF16),
            pltpu.VMEM((POOL_GROUPS, tm, tm), BF16),
            pltpu.VMEM((POOL_GROUPS, tm, MAX_HALO), BF16),
        ],
        compiler_params=_cparams(("arbitrary",), 48),
        name="mix",
    )(attn, proj, proj, b_gate, wap_bf, wpp_bf, pw_bf, pool_scale)


META_E, META_POS, META_W = 0, TOP_K, 2 * TOP_K


def _post_kernel(x_ref, m_ref, wout_ref, g_ref, rwh_ref, rwl_ref, rb_ref,
                 x1_ref, h_ref, meta_ref, cnt_ref, carry_ref, tri_ref):
    i = pl.program_id(0)
    tm = TM_POST

    @pl.when(i == 0)
    def _():
        carry_ref[...] = jnp.zeros_like(carry_ref)
        t = lax.broadcasted_iota(jnp.int32, (tm, tm), 0)
        j = lax.broadcasted_iota(jnp.int32, (tm, tm), 1)
        tri_ref[...] = jnp.where(j < t, 1.0, 0.0).astype(BF16)

    x1 = x_ref[...] + jnp.dot(m_ref[...], wout_ref[...], preferred_element_type=F32)
    x1_ref[...] = x1
    var = jnp.mean(x1 * x1, axis=-1, keepdims=True)
    h = x1 * lax.rsqrt(var + RMS_EPS) * g_ref[...]
    h_ref[...] = h

    h_hi = h.astype(BF16)
    h_lo = (h - h_hi.astype(F32)).astype(BF16)
    logits = (jnp.dot(h_hi, rwh_ref[...], preferred_element_type=F32)
              + jnp.dot(h_hi, rwl_ref[...], preferred_element_type=F32)
              + jnp.dot(h_lo, rwh_ref[...], preferred_element_type=F32)
              + rb_ref[...])

    lane_e = lax.broadcasted_iota(jnp.int32, (tm, N_EXPERTS), 1).astype(F32)
    work = logits
    sel = jnp.zeros((tm, N_EXPERTS), F32)
    picks = []
    for _ in range(TOP_K):
        m = jnp.max(work, axis=-1, keepdims=True)
        idx = jnp.min(jnp.where(work == m, lane_e, float(N_EXPERTS)), axis=-1, keepdims=True)
        onehot = lane_e == idx
        picks.append((m, idx, onehot))
        sel = jnp.where(onehot, 1.0, sel)
        work = jnp.where(onehot, -jnp.inf, work)

    rank = jnp.dot(tri_ref[...], sel.astype(BF16), preferred_element_type=F32) + carry_ref[...]
    carry_ref[...] += jnp.sum(sel, axis=0, keepdims=True)
    cnt_ref[...] = carry_ref[...]

    top = picks[0][0]
    exps = [jnp.exp(m - top) for m, _, _ in picks]
    denom = exps[0] + exps[1] + exps[2] + exps[3]
    lane = lax.broadcasted_iota(jnp.int32, (tm, LANES), 1)
    meta = jnp.zeros((tm, LANES), F32)
    for k, (m, idx, onehot) in enumerate(picks):
        pos_k = jnp.sum(jnp.where(onehot, rank, 0.0), axis=-1, keepdims=True)
        meta = jnp.where(lane == META_E + k, idx, meta)
        meta = jnp.where(lane == META_POS + k, pos_k, meta)
        meta = jnp.where(lane == META_W + k, exps[k] / denom, meta)
    meta_ref[...] = meta


def _post(x2, merged, wout_bf, g, rw_hi, rw_lo, rb):
    s = x2.shape[0]
    tm = TM_POST
    return pl.pallas_call(
        _post_kernel,
        grid=(s // tm,),
        in_specs=[
            pl.BlockSpec((tm, D_MODEL), lambda i: (i, 0)),
            pl.BlockSpec((tm, D_MODEL), lambda i: (i, 0)),
            _resident((D_MODEL, D_MODEL), lambda i: (0, 0)),
            pl.BlockSpec((1, D_MODEL), lambda i: (0, 0)),
            pl.BlockSpec((D_MODEL, N_EXPERTS), lambda i: (0, 0)),
            pl.BlockSpec((D_MODEL, N_EXPERTS), lambda i: (0, 0)),
            pl.BlockSpec((1, N_EXPERTS), lambda i: (0, 0)),
        ],
        out_specs=[
            pl.BlockSpec((tm, D_MODEL), lambda i: (i, 0)),
            pl.BlockSpec((tm, D_MODEL), lambda i: (i, 0)),
            pl.BlockSpec((tm, LANES), lambda i: (i, 0)),
            pl.BlockSpec((1, N_EXPERTS), lambda i: (0, 0)),
        ],
        out_shape=[
            jax.ShapeDtypeStruct((s, D_MODEL), F32),
            jax.ShapeDtypeStruct((s, D_MODEL), F32),
            jax.ShapeDtypeStruct((s, LANES), F32),
            jax.ShapeDtypeStruct((1, N_EXPERTS), F32),
        ],
        scratch_shapes=[
            pltpu.VMEM((1, N_EXPERTS), F32),
            pltpu.VMEM((tm, tm), BF16),
        ],
        compiler_params=_cparams(("arbitrary",), 48),
        name="post",
    )(x2, merged, wout_bf, g, rw_hi, rw_lo, rb)


def _row_copy(src, src_row, dst, dst_row, sem):
    return pltpu.make_async_copy(src.at[pl.ds(src_row, 1)], dst.at[pl.ds(dst_row, 1)], sem)


def _block_copy(src, dst, dst_block, sem):
    return pltpu.make_async_copy(src, dst.at[pl.ds(dst_block * EXPERT_BLOCK, EXPERT_BLOCK)], sem)


def _dispatch_kernel(padbase_ref, padcnt_ref, nreal_ref, dest_ref, h_ref, zero_hbm, xs_hbm,
                     sem, pad_sem):
    i = pl.program_id(0)
    tm = TM_DISP
    n_blocks = xs_hbm.shape[0] // EXPERT_BLOCK

    def issue(t, c):
        for k in range(TOP_K):
            _row_copy(h_ref, t, xs_hbm, dest_ref[0, 0, TOP_K * t + k], sem).start()
        return c

    lax.fori_loop(0, tm, issue, 0, unroll=8)

    @pl.when(i == 0)
    def _():
        for e in range(N_EXPERTS):
            def fill(r, c, e=e):
                _row_copy(zero_hbm, 0, xs_hbm, padbase_ref[e] + r, pad_sem).start()
                return c
            lax.fori_loop(0, padcnt_ref[e], fill, 0)

        def fill_block(b, c):
            _block_copy(zero_hbm, xs_hbm, b, pad_sem).start()
            return c
        lax.fori_loop(nreal_ref[0], n_blocks, fill_block, 0)

        for e in range(N_EXPERTS):
            def drain(r, c):
                _row_copy(zero_hbm, 0, xs_hbm, 0, pad_sem).wait()
                return c
            lax.fori_loop(0, padcnt_ref[e], drain, 0)

        def drain_block(b, c):
            _block_copy(zero_hbm, xs_hbm, 0, pad_sem).wait()
            return c
        lax.fori_loop(nreal_ref[0], n_blocks, drain_block, 0)

    for _ in range(TOP_K * tm // DRAIN_ROWS):
        pltpu.make_async_copy(h_ref.at[pl.ds(0, DRAIN_ROWS)], xs_hbm.at[pl.ds(0, DRAIN_ROWS)],
                              sem).wait()


def _dispatch(padbase, padcnt, nreal, dest, h, n_rows):
    s = h.shape[0]
    tm = TM_DISP
    dest3 = dest.reshape(s // tm, 1, TOP_K * tm)
    zero_rows = jnp.zeros((EXPERT_BLOCK, D_MODEL), F32)
    grid_spec = pltpu.PrefetchScalarGridSpec(
        num_scalar_prefetch=3,
        grid=(s // tm,),
        in_specs=[
            pl.BlockSpec((1, 1, TOP_K * tm), lambda i, pb, pc, nr: (i, 0, 0),
                         memory_space=pltpu.SMEM),
            pl.BlockSpec((tm, D_MODEL), lambda i, pb, pc, nr: (i, 0)),
            pl.BlockSpec(memory_space=pl.ANY),
        ],
        out_specs=pl.BlockSpec(memory_space=pl.ANY),
        scratch_shapes=[pltpu.SemaphoreType.DMA, pltpu.SemaphoreType.DMA],
    )
    return pl.pallas_call(
        _dispatch_kernel,
        grid_spec=grid_spec,
        out_shape=jax.ShapeDtypeStruct((n_rows, D_MODEL), F32),
        compiler_params=_cparams(("arbitrary",), 32),
        name="dispatch",
    )(padbase, padcnt, nreal, dest3, h, zero_rows)


def _up_kernel(be_ref, nreal_ref, x_ref, wg_ref, wu_ref, bg_ref, bu_ref, a_ref):
    b = pl.program_id(0)

    @pl.when(b < nreal_ref[0])
    def _():
        x = x_ref[...].astype(BF16)
        g = jnp.dot(x, wg_ref[0], preferred_element_type=F32) + bg_ref[0]
        u = jnp.dot(x, wu_ref[0], preferred_element_type=F32) + bu_ref[0]
        g = jnp.minimum(g, SWIGLU_LIMIT)
        u = jnp.clip(u, -SWIGLU_LIMIT, SWIGLU_LIMIT)
        a_ref[...] = (g * jax.nn.sigmoid(SWIGLU_ALPHA * g) * (u + 1.0)).astype(a_ref.dtype)

    @pl.when(b >= nreal_ref[0])
    def _():
        a_ref[...] = jnp.zeros_like(a_ref)


def _down_kernel(be_ref, nreal_ref, a_ref, w_ref, b_ref, y_ref, wbf_ref):
    b = pl.program_id(0)

    @pl.when((b == 0) | (be_ref[b] != be_ref[jnp.maximum(b - 1, 0)]))
    def _():
        wbf_ref[...] = w_ref[0].astype(BF16)

    @pl.when(b < nreal_ref[0])
    def _():
        y_ref[...] = jnp.dot(a_ref[...], wbf_ref[...], preferred_element_type=F32) + b_ref[0]

    @pl.when(b >= nreal_ref[0])
    def _():
        y_ref[...] = jnp.zeros_like(y_ref)


def _row_block(b, be, nr):
    return (jnp.minimum(b, nr[0] - 1), 0)


def _expert_block(b, be, nr):
    return (be[b], 0, 0)


def _w1_prep_kernel(w_ref, p_ref, g_ref, u_ref):
    w = w_ref[0].astype(BF16)
    r = jnp.dot(w, p_ref[...], preferred_element_type=F32)
    half = W1_TILE // 2
    g_ref[0] = r[:, :half].astype(BF16)
    u_ref[0] = r[:, half:].astype(BF16)


def _w1_prep(w1):
    half = W1_TILE // 2
    i = jnp.arange(W1_TILE)[:, None]
    c = jnp.arange(W1_TILE)[None, :]
    perm = jnp.where(c < half, i == 2 * c, i == 2 * (c - half) + 1).astype(BF16)
    out = jax.ShapeDtypeStruct((N_EXPERTS, D_MODEL, D_FF), BF16)
    return pl.pallas_call(
        _w1_prep_kernel,
        grid=(N_EXPERTS, 2 * D_FF // W1_TILE),
        in_specs=[
            pl.BlockSpec((1, D_MODEL, W1_TILE), lambda e, j: (e, 0, j)),
            pl.BlockSpec((W1_TILE, W1_TILE), lambda e, j: (0, 0)),
        ],
        out_specs=[
            pl.BlockSpec((1, D_MODEL, half), lambda e, j: (e, 0, j)),
            pl.BlockSpec((1, D_MODEL, half), lambda e, j: (e, 0, j)),
        ],
        out_shape=[out, out],
        compiler_params=_cparams(("parallel", "parallel"), 32),
        name="w1_prep",
    )(w1, perm)


def _experts(block_e, nreal, xs, w1g, w1u, b1g, b1u, w2, b2):
    n_rows = xs.shape[0]
    nb = n_rows // EXPERT_BLOCK
    act = pl.pallas_call(
        _up_kernel,
        grid_spec=pltpu.PrefetchScalarGridSpec(
            num_scalar_prefetch=2,
            grid=(nb,),
            in_specs=[
                pl.BlockSpec((EXPERT_BLOCK, D_MODEL), _row_block),
                pl.BlockSpec((1, D_MODEL, D_FF), _expert_block),
                pl.BlockSpec((1, D_MODEL, D_FF), _expert_block),
                pl.BlockSpec((1, 1, D_FF), _expert_block),
                pl.BlockSpec((1, 1, D_FF), _expert_block),
            ],
            out_specs=pl.BlockSpec((EXPERT_BLOCK, D_FF), lambda b, be, nr: (b, 0)),
        ),
        out_shape=jax.ShapeDtypeStruct((n_rows, D_FF), BF16),
        compiler_params=_cparams(("arbitrary",), 56),
        name="expert_up",
    )(block_e, nreal, xs, w1g, w1u, b1g, b1u)
    return pl.pallas_call(
        _down_kernel,
        grid_spec=pltpu.PrefetchScalarGridSpec(
            num_scalar_prefetch=2,
            grid=(nb,),
            in_specs=[
                pl.BlockSpec((EXPERT_BLOCK, D_FF), _row_block),
                pl.BlockSpec((1, D_FF, D_MODEL), _expert_block),
                pl.BlockSpec((1, 1, D_MODEL), _expert_block),
            ],
            out_specs=pl.BlockSpec((EXPERT_BLOCK, D_MODEL), lambda b, be, nr: (b, 0)),
            scratch_shapes=[pltpu.VMEM((D_FF, D_MODEL), BF16)],
        ),
        out_shape=jax.ShapeDtypeStruct((n_rows, D_MODEL), F32),
        compiler_params=_cparams(("arbitrary",), 56),
        name="expert_down",
    )(block_e, nreal, act, w2, b2)


def _combine_kernel(dcur_ref, dnxt_ref, x1_ref, meta_ref, g_ref, y_hbm, o_ref, ybuf, sem):
    i = pl.program_id(0)
    n = pl.num_programs(0)
    tm = TM_COMB
    slot = i % 2

    def issue(dref, s):
        def body(t, c):
            for k in range(TOP_K):
                pltpu.make_async_copy(y_hbm.at[pl.ds(dref[0, 0, TOP_K * t + k], 1)],
                                      ybuf.at[s, k, pl.ds(t, 1)], sem.at[s]).start()
            return c
        lax.fori_loop(0, tm, body, 0, unroll=8)

    @pl.when(i == 0)
    def _():
        issue(dcur_ref, 0)

    @pl.when(i + 1 < n)
    def _():
        issue(dnxt_ref, 1 - slot)

    for k in range(TOP_K):
        pltpu.make_async_copy(y_hbm.at[pl.ds(0, tm)], ybuf.at[slot, k], sem.at[slot]).wait()

    meta = meta_ref[...]
    acc = x1_ref[...]
    for k in range(TOP_K):
        acc = acc + meta[:, META_W + k:META_W + k + 1] * ybuf[slot, k]
    var = jnp.mean(acc * acc, axis=-1, keepdims=True)
    o_ref[...] = (acc * lax.rsqrt(var + RMS_EPS) * g_ref[...]).astype(o_ref.dtype)


def _combine(dest, x1, meta, g, y):
    s = x1.shape[0]
    tm = TM_COMB
    nt = s // tm
    dest3 = dest.reshape(nt, 1, TOP_K * tm)
    return pl.pallas_call(
        _combine_kernel,
        grid=(nt,),
        in_specs=[
            pl.BlockSpec((1, 1, TOP_K * tm), lambda i: (i, 0, 0), memory_space=pltpu.SMEM),
            pl.BlockSpec((1, 1, TOP_K * tm), lambda i: (jnp.minimum(i + 1, nt - 1), 0, 0),
                         memory_space=pltpu.SMEM),
            pl.BlockSpec((tm, D_MODEL), lambda i: (i, 0)),
            pl.BlockSpec((tm, LANES), lambda i: (i, 0)),
            pl.BlockSpec((1, D_MODEL), lambda i: (0, 0)),
            pl.BlockSpec(memory_space=pl.ANY),
        ],
        out_specs=pl.BlockSpec((tm, D_MODEL), lambda i: (i, 0)),
        out_shape=jax.ShapeDtypeStruct((s, D_MODEL), F32),
        scratch_shapes=[
            pltpu.VMEM((2, TOP_K, tm, D_MODEL), F32),
            pltpu.SemaphoreType.DMA((2,)),
        ],
        compiler_params=_cparams(("arbitrary",), 40),
        name="combine",
    )(dest3, dest3, x1, meta, g, y)


def _routing_tables(meta, cnt, n_blocks):
    counts = cnt[0].astype(jnp.int32)
    padded = ((counts + EXPERT_BLOCK - 1) // EXPERT_BLOCK) * EXPERT_BLOCK
    pad_end = jnp.cumsum(padded)
    pad_start = pad_end - padded
    e4 = meta[:, META_E:META_E + TOP_K].astype(jnp.int32)
    pos4 = meta[:, META_POS:META_POS + TOP_K].astype(jnp.int32)
    experts = jnp.arange(N_EXPERTS, dtype=jnp.int32)
    dest = pos4 + jnp.sum(jnp.where(e4[..., None] == experts, pad_start, 0), axis=-1)
    nreal = pad_end[-1:] // EXPERT_BLOCK
    blk = jnp.minimum(jnp.arange(n_blocks, dtype=jnp.int32), nreal[0] - 1)
    block_e = jnp.sum(pad_end[None, :] <= (blk * EXPERT_BLOCK)[:, None], axis=1).astype(jnp.int32)
    return dest, block_e, nreal.astype(jnp.int32), (pad_start + counts).astype(jnp.int32), \
        (padded - counts).astype(jnp.int32)


def kernel(x, norm_mix, w_in, b_gate, rel_bias, w_attn_proj, pool_w, pool_scale, w_pool_proj,
           w_out, norm_ffn, router_w, router_b, w1, b1, w2, b2, norm_final):
    bsz, seq, d = x.shape
    n_tok = bsz * seq
    assert w_in.shape[0] == 1 and bsz == 1 and d == D_MODEL and seq % TM_IN == 0
    n_blocks = -(-(n_tok * TOP_K) // EXPERT_BLOCK) + N_EXPERTS
    n_rows = n_blocks * EXPERT_BLOCK
    x2 = x.reshape(n_tok, d)
    row = lambda v: v.reshape(1, -1).astype(F32)
    l = 0
    proj = _inproj(x2, row(norm_mix[l]), w_in[l].astype(BF16))
    attn = _attention(proj, _attn_bias_table(rel_bias[l]))
    merged = _mix(attn, proj, row(b_gate[l]), w_attn_proj[l].astype(BF16),
                  w_pool_proj[l].astype(BF16), pool_w[l].astype(BF16), row(pool_scale[l]))
    rw = router_w[l].astype(F32)
    rw_hi = rw.astype(BF16)
    rw_lo = (rw - rw_hi.astype(F32)).astype(BF16)
    x1, h, meta, cnt = _post(x2, merged, w_out[l].astype(BF16), row(norm_ffn[l]),
                             rw_hi, rw_lo, row(router_b[l]))
    dest, block_e, nreal, padbase, padcnt = _routing_tables(meta, cnt, n_blocks)
    xs = _dispatch(padbase, padcnt, nreal, dest, h, n_rows)
    w1g, w1u = _w1_prep(w1[l])
    y = _experts(block_e, nreal, xs, w1g, w1u,
                 b1[l][:, None, 0::2].astype(F32), b1[l][:, None, 1::2].astype(F32),
                 w2[l], b2[l][:, None, :].astype(F32))
    out = _combine(dest, x1, meta, row(norm_final), y)
    return out.reshape(bsz, seq, d)
```

```python
import functools

import jax
import jax.numpy as jnp
from jax import lax
from jax.experimental import pallas as pl
from jax.experimental.pallas import tpu as pltpu

D_MODEL = 2048
CHUNK = 64
LEFT_CHUNKS = 8
BAND = (LEFT_CHUNKS + 1) * CHUNK
ATTN_WIDTH = D_MODEL // 2
HEAD_DIM = 64
ATTN_HEADS = ATTN_WIDTH // HEAD_DIM
MAX_REL = 256
POOL_WINDOWS = (2, 4, 8, 16)
POOL_GROUPS = len(POOL_WINDOWS)
POOL_WIDTH = D_MODEL // 2
POOL_GROUP_DIM = POOL_WIDTH // POOL_GROUPS
N_BRANCHES = 2
IN_WIDTH = 3 * ATTN_WIDTH + POOL_WIDTH + N_BRANCHES * D_MODEL
N_EXPERTS = 32
TOP_K = 4
D_FF = D_MODEL
SWIGLU_LIMIT = 7.0
SWIGLU_ALPHA = 1.702
EXPERT_BLOCK = 256
RMS_EPS = 1e-5
NEG_INF = -1e30

LANES = 128
MAX_HALO = max(POOL_WINDOWS)

TM_IN, TN_IN = 1024, 1024
Q_GROUP = 4
TQ = Q_GROUP * CHUNK
KV_BLOCKS = LEFT_CHUNKS // Q_GROUP + 1
KV_SPAN = KV_BLOCKS * TQ
TM_MIX = 512
TM_POST = 512
TM_DISP = 512
DRAIN_ROWS = 512
W1_TILE = 512
W1_TILES_PER_STEP = 2
TM_COMB = 256

MIB = 1024 * 1024
F32 = jnp.float32
BF16 = jnp.bfloat16


def _cparams(sem, vmem_mib):
    return pltpu.CompilerParams(dimension_semantics=sem, vmem_limit_bytes=vmem_mib * MIB)


def _resident(shape, index_map):
    return pl.BlockSpec(shape, index_map, pipeline_mode=pl.Buffered(1))


def _inproj_kernel(x_ref, g_ref, w_ref, o_ref, xn_ref):
    @pl.when(pl.program_id(1) == 0)
    def _():
        x = x_ref[...]
        var = jnp.mean(x * x, axis=-1, keepdims=True)
        xn_ref[...] = (x * lax.rsqrt(var + RMS_EPS) * g_ref[...]).astype(BF16)

    o_ref[...] = jnp.dot(xn_ref[...], w_ref[...], preferred_element_type=F32).astype(o_ref.dtype)


def _inproj(x2, g, w_bf):
    s = x2.shape[0]
    return pl.pallas_call(
        _inproj_kernel,
        grid=(s // TM_IN, IN_WIDTH // TN_IN),
        in_specs=[
            pl.BlockSpec((TM_IN, D_MODEL), lambda i, j: (i, 0)),
            pl.BlockSpec((1, D_MODEL), lambda i, j: (0, 0)),
            pl.BlockSpec((D_MODEL, TN_IN), lambda i, j: (0, j)),
        ],
        out_specs=pl.BlockSpec((TM_IN, TN_IN), lambda i, j: (i, j)),
        out_shape=jax.ShapeDtypeStruct((s, IN_WIDTH), BF16),
        scratch_shapes=[pltpu.VMEM((TM_IN, D_MODEL), BF16)],
        compiler_params=_cparams(("parallel", "arbitrary"), 48),
        name="inproj",
    )(x2, g, w_bf)


def _attn_kernel(q_ref, k0_ref, k1_ref, k2_ref, v0_ref, v1_ref, v2_ref, bias_ref, o_ref):
    i = pl.program_id(0)
    k_refs = (k0_ref, k1_ref, k2_ref)
    v_refs = (v0_ref, v1_ref, v2_ref)
    col = lax.broadcasted_iota(jnp.int32, (1, KV_SPAN), 1)
    valid = col >= (LEFT_CHUNKS * CHUNK - TQ * i)
    lane = lax.broadcasted_iota(jnp.int32, (1, LANES), 1)
    scale = HEAD_DIM ** -0.5
    nt = (((1,), (1,)), ((), ()))
    for hp in range(ATTN_HEADS // 2):
        cs = slice(hp * LANES, (hp + 1) * LANES)
        qp = q_ref[:, cs]
        ks = [r[:, cs] for r in k_refs]
        vs = [r[:, cs] for r in v_refs]
        outs = []
        for hh in range(2):
            head_lanes = (lane >= hh * HEAD_DIM) & (lane < (hh + 1) * HEAD_DIM)
            qh = jnp.where(head_lanes, qp, jnp.zeros_like(qp))
            s = jnp.concatenate(
                [lax.dot_general(qh, kb, nt, preferred_element_type=F32) for kb in ks], axis=1)
            s = s * scale + bias_ref[2 * hp + hh]
            s = jnp.where(valid, s, NEG_INF)
            m = jnp.max(s, axis=-1, keepdims=True)
            p = jnp.exp(s - m)
            l = jnp.sum(p, axis=-1, keepdims=True)
            pb = p.astype(BF16)
            o = jnp.dot(pb[:, 0:TQ], vs[0], preferred_element_type=F32)
            o += jnp.dot(pb[:, TQ:2 * TQ], vs[1], preferred_element_type=F32)
            o += jnp.dot(pb[:, 2 * TQ:3 * TQ], vs[2], preferred_element_type=F32)
            outs.append(o / l)
        o_ref[:, cs] = jnp.where(lane < HEAD_DIM, outs[0], outs[1]).astype(o_ref.dtype)


def _attn_bias_table(rel_bias):
    dist = LEFT_CHUNKS * CHUNK + (CHUNK - 1) - jnp.arange(BAND + CHUNK - 1)
    by_col = rel_bias.astype(F32)[:, jnp.clip(dist, -(CHUNK - 1), MAX_REL) + (CHUNK - 1)]
    band_bias = jnp.stack(
        [by_col[:, CHUNK - 1 - i:CHUNK - 1 - i + BAND] for i in range(CHUNK)], axis=1)
    rows = [
        jnp.pad(band_bias, ((0, 0), (0, 0), (c * CHUNK, KV_SPAN - BAND - c * CHUNK)),
                constant_values=NEG_INF)
        for c in range(Q_GROUP)
    ]
    return jnp.concatenate(rows, axis=1)


def _attention(proj, bias_tab):
    s = proj.shape[0]
    kcol, vcol = 1, 2

    def kv_spec(back, colblk):
        return pl.BlockSpec((TQ, ATTN_WIDTH), lambda i: (jnp.maximum(i - back, 0), colblk))

    return pl.pallas_call(
        _attn_kernel,
        grid=(s // TQ,),
        in_specs=[
            pl.BlockSpec((TQ, ATTN_WIDTH), lambda i: (i, 0)),
            kv_spec(2, kcol), kv_spec(1, kcol), kv_spec(0, kcol),
            kv_spec(2, vcol), kv_spec(1, vcol), kv_spec(0, vcol),
            _resident((ATTN_HEADS, TQ, KV_SPAN), lambda i: (0, 0, 0)),
        ],
        out_specs=pl.BlockSpec((TQ, ATTN_WIDTH), lambda i: (i, 0)),
        out_shape=jax.ShapeDtypeStruct((s, ATTN_WIDTH), BF16),
        compiler_params=_cparams(("parallel",), 48),
        name="attn",
    )(proj, proj, proj, proj, proj, proj, proj, bias_tab)


def _mix_kernel(attn_ref, pin_ref, gl_ref, bg_ref, wap_ref, wpp_ref, pw_ref, ps_ref,
                o_ref, halo_ref, win_ref, hwin_ref):
    i = pl.program_id(0)
    tm = TM_MIX

    @pl.when(i == 0)
    def _():
        halo_ref[...] = jnp.zeros_like(halo_ref)
        t = lax.broadcasted_iota(jnp.int32, (tm, tm), 0)
        j = lax.broadcasted_iota(jnp.int32, (tm, tm), 1)
        th = lax.broadcasted_iota(jnp.int32, (tm, MAX_HALO), 0)
        jh = lax.broadcasted_iota(jnp.int32, (tm, MAX_HALO), 1)
        for gi, w in enumerate(POOL_WINDOWS):
            win_ref[gi] = jnp.where((t - j >= 0) & (t - j < w), 1.0, 0.0).astype(BF16)
            hwin_ref[gi] = jnp.where(th + MAX_HALO - jh < w, 1.0, 0.0).astype(BF16)

    pos = i * tm + lax.broadcasted_iota(jnp.int32, (tm, 1), 0)
    u_all = pin_ref[...]
    halo = halo_ref[...]
    mixed = []
    for gi, w in enumerate(POOL_WINDOWS):
        cs = slice(gi * POOL_GROUP_DIM, (gi + 1) * POOL_GROUP_DIM)
        u = u_all[:, cs]
        wsum = jnp.dot(win_ref[gi], u, preferred_element_type=F32)
        wsum += jnp.dot(hwin_ref[gi], halo[:, cs], preferred_element_type=F32)
        inv_cnt = jnp.where(pos + 1 >= w, 1.0 / w, 1.0 / jnp.minimum(pos + 1, w).astype(F32))
        pooled = wsum * inv_cnt - u.astype(F32)
        mixed.append(jnp.dot(pooled.astype(BF16), pw_ref[gi], preferred_element_type=F32))
    mixed = jnp.concatenate(mixed, axis=1) * ps_ref[...]
    halo_ref[...] = u_all[tm - MAX_HALO:, :]

    y_pool = jnp.dot(mixed.astype(BF16), wpp_ref[...], preferred_element_type=F32)
    y_attn = jnp.dot(attn_ref[...], wap_ref[...], preferred_element_type=F32)
    gates = jax.nn.sigmoid(gl_ref[...].astype(F32) + bg_ref[...])
    merged = gates[:, :D_MODEL] * y_attn + gates[:, D_MODEL:] * y_pool
    o_ref[...] = merged.astype(o_ref.dtype)


def _mix(attn, proj, b_gate, wap_bf, wpp_bf, pw_bf, pool_scale):
    s = attn.shape[0]
    tm = TM_MIX
    pin_col = 3 * ATTN_WIDTH // POOL_WIDTH
    gl_col = (3 * ATTN_WIDTH + POOL_WIDTH) // (N_BRANCHES * D_MODEL)
    return pl.pallas_call(
        _mix_kernel,
        grid=(s // tm,),
        in_specs=[
            pl.BlockSpec((tm, ATTN_WIDTH), lambda i: (i, 0)),
            pl.BlockSpec((tm, POOL_WIDTH), lambda i: (i, pin_col)),
            pl.BlockSpec((tm, N_BRANCHES * D_MODEL), lambda i: (i, gl_col)),
            pl.BlockSpec((1, N_BRANCHES * D_MODEL), lambda i: (0, 0)),
            _resident((ATTN_WIDTH, D_MODEL), lambda i: (0, 0)),
            _resident((POOL_WIDTH, D_MODEL), lambda i: (0, 0)),
            _resident((POOL_GROUPS, POOL_GROUP_DIM, POOL_GROUP_DIM), lambda i: (0, 0, 0)),
            pl.BlockSpec((1, POOL_WIDTH), lambda i: (0, 0)),
        ],
        out_specs=pl.BlockSpec((tm, D_MODEL), lambda i: (i, 0)),
        out_shape=jax.ShapeDtypeStruct((s, D_MODEL), BF16),
        scratch_shapes=[
            pltpu.VMEM((MAX_HALO, POOL_WIDTH), BF16),
            pltpu.VMEM((POOL_GROUPS, tm, tm), BF16),
            pltpu.VMEM((POOL_GROUPS, tm, MAX_HALO), BF16),
        ],
        compiler_params=_cparams(("arbitrary",), 48),
        name="mix",
    )(attn, proj, proj, b_gate, wap_bf, wpp_bf, pw_bf, pool_scale)


META_E, META_POS, META_W = 0, TOP_K, 2 * TOP_K


def _post_kernel(x_ref, m_ref, wout_ref, g_ref, rwh_ref, rwl_ref, rb_ref,
                 x1_ref, h_ref, meta_ref, cnt_ref, carry_ref, tri_ref):
    i = pl.program_id(0)
    tm = TM_POST

    @pl.when(i == 0)
    def _():
        carry_ref[...] = jnp.zeros_like(carry_ref)
        t = lax.broadcasted_iota(jnp.int32, (tm, tm), 0)
        j = lax.broadcasted_iota(jnp.int32, (tm, tm), 1)
        tri_ref[...] = jnp.where(j < t, 1.0, 0.0).astype(BF16)

    x1 = x_ref[...] + jnp.dot(m_ref[...], wout_ref[...], preferred_element_type=F32)
    x1_ref[...] = x1
    var = jnp.mean(x1 * x1, axis=-1, keepdims=True)
    h = x1 * lax.rsqrt(var + RMS_EPS) * g_ref[...]
    h_ref[...] = h

    h_hi = h.astype(BF16)
    h_lo = (h - h_hi.astype(F32)).astype(BF16)
    logits = (jnp.dot(h_hi, rwh_ref[...], preferred_element_type=F32)
              + jnp.dot(h_hi, rwl_ref[...], preferred_element_type=F32)
              + jnp.dot(h_lo, rwh_ref[...], preferred_element_type=F32)
              + rb_ref[...])

    lane_e = lax.broadcasted_iota(jnp.int32, (tm, N_EXPERTS), 1).astype(F32)
    work = logits
    sel = jnp.zeros((tm, N_EXPERTS), F32)
    picks = []
    for _ in range(TOP_K):
        m = jnp.max(work, axis=-1, keepdims=True)
        idx = jnp.min(jnp.where(work == m, lane_e, float(N_EXPERTS)), axis=-1, keepdims=True)
        onehot = lane_e == idx
        picks.append((m, idx, onehot))
        sel = jnp.where(onehot, 1.0, sel)
        work = jnp.where(onehot, -jnp.inf, work)

    rank = jnp.dot(tri_ref[...], sel.astype(BF16), preferred_element_type=F32) + carry_ref[...]
    carry_ref[...] += jnp.sum(sel, axis=0, keepdims=True)
    cnt_ref[...] = carry_ref[...]

    top = picks[0][0]
    exps = [jnp.exp(m - top) for m, _, _ in picks]
    denom = exps[0] + exps[1] + exps[2] + exps[3]
    lane = lax.broadcasted_iota(jnp.int32, (tm, LANES), 1)
    meta = jnp.zeros((tm, LANES), F32)
    for k, (m, idx, onehot) in enumerate(picks):
        pos_k = jnp.sum(jnp.where(onehot, rank, 0.0), axis=-1, keepdims=True)
        meta = jnp.where(lane == META_E + k, idx, meta)
        meta = jnp.where(lane == META_POS + k, pos_k, meta)
        meta = jnp.where(lane == META_W + k, exps[k] / denom, meta)
    meta_ref[...] = meta


def _post(x2, merged, wout_bf, g, rw_hi, rw_lo, rb):
    s = x2.shape[0]
    tm = TM_POST
    return pl.pallas_call(
        _post_kernel,
        grid=(s // tm,),
        in_specs=[
            pl.BlockSpec((tm, D_MODEL), lambda i: (i, 0)),
            pl.BlockSpec((tm, D_MODEL), lambda i: (i, 0)),
            _resident((D_MODEL, D_MODEL), lambda i: (0, 0)),
            pl.BlockSpec((1, D_MODEL), lambda i: (0, 0)),
            pl.BlockSpec((D_MODEL, N_EXPERTS), lambda i: (0, 0)),
            pl.BlockSpec((D_MODEL, N_EXPERTS), lambda i: (0, 0)),
            pl.BlockSpec((1, N_EXPERTS), lambda i: (0, 0)),
        ],
        out_specs=[
            pl.BlockSpec((tm, D_MODEL), lambda i: (i, 0)),
            pl.BlockSpec((tm, D_MODEL), lambda i: (i, 0)),
            pl.BlockSpec((tm, LANES), lambda i: (i, 0)),
            pl.BlockSpec((1, N_EXPERTS), lambda i: (0, 0)),
        ],
        out_shape=[
            jax.ShapeDtypeStruct((s, D_MODEL), F32),
            jax.ShapeDtypeStruct((s, D_MODEL), F32),
            jax.ShapeDtypeStruct((s, LANES), F32),
            jax.ShapeDtypeStruct((1, N_EXPERTS), F32),
        ],
        scratch_shapes=[
            pltpu.VMEM((1, N_EXPERTS), F32),
            pltpu.VMEM((tm, tm), BF16),
        ],
        compiler_params=_cparams(("arbitrary",), 48),
        name="post",
    )(x2, merged, wout_bf, g, rw_hi, rw_lo, rb)


def _row_copy(src, src_row, dst, dst_row, sem):
    return pltpu.make_async_copy(src.at[pl.ds(src_row, 1)], dst.at[pl.ds(dst_row, 1)], sem)


def _block_copy(src, dst, dst_block, sem):
    return pltpu.make_async_copy(src, dst.at[pl.ds(dst_block * EXPERT_BLOCK, EXPERT_BLOCK)], sem)


def _dispatch_kernel(padbase_ref, padcnt_ref, nreal_ref, dest_ref, h_ref, zero_hbm, xs_hbm,
                     sem, pad_sem):
    i = pl.program_id(0)
    tm = TM_DISP
    n_blocks = xs_hbm.shape[0] // EXPERT_BLOCK

    for t in range(tm):
        for k in range(TOP_K):
            _row_copy(h_ref, t, xs_hbm, dest_ref[0, 0, TOP_K * t + k], sem).start()

    @pl.when(i == 0)
    def _():
        for e in range(N_EXPERTS):
            def fill(r, c, e=e):
                _row_copy(zero_hbm, 0, xs_hbm, padbase_ref[e] + r, pad_sem).start()
                return c
            lax.fori_loop(0, padcnt_ref[e], fill, 0)

        def fill_block(b, c):
            _block_copy(zero_hbm, xs_hbm, b, pad_sem).start()
            return c
        lax.fori_loop(nreal_ref[0], n_blocks, fill_block, 0)

        for e in range(N_EXPERTS):
            def drain(r, c):
                _row_copy(zero_hbm, 0, xs_hbm, 0, pad_sem).wait()
                return c
            lax.fori_loop(0, padcnt_ref[e], drain, 0)

        def drain_block(b, c):
            _block_copy(zero_hbm, xs_hbm, 0, pad_sem).wait()
            return c
        lax.fori_loop(nreal_ref[0], n_blocks, drain_block, 0)

    for _ in range(TOP_K * tm // DRAIN_ROWS):
        pltpu.make_async_copy(h_ref.at[pl.ds(0, DRAIN_ROWS)], xs_hbm.at[pl.ds(0, DRAIN_ROWS)],
                              sem).wait()


def _dispatch(padbase, padcnt, nreal, dest, h, n_rows):
    s = h.shape[0]
    tm = TM_DISP
    dest3 = dest.reshape(s // tm, 1, TOP_K * tm)
    zero_rows = jnp.zeros((EXPERT_BLOCK, D_MODEL), F32)
    grid_spec = pltpu.PrefetchScalarGridSpec(
        num_scalar_prefetch=3,
        grid=(s // tm,),
        in_specs=[
            pl.BlockSpec((1, 1, TOP_K * tm), lambda i, pb, pc, nr: (i, 0, 0),
                         memory_space=pltpu.SMEM),
            pl.BlockSpec((tm, D_MODEL), lambda i, pb, pc, nr: (i, 0)),
            pl.BlockSpec(memory_space=pl.ANY),
        ],
        out_specs=pl.BlockSpec(memory_space=pl.ANY),
        scratch_shapes=[pltpu.SemaphoreType.DMA, pltpu.SemaphoreType.DMA],
    )
    return pl.pallas_call(
        _dispatch_kernel,
        grid_spec=grid_spec,
        out_shape=jax.ShapeDtypeStruct((n_rows, D_MODEL), F32),
        compiler_params=_cparams(("arbitrary",), 32),
        name="dispatch",
    )(padbase, padcnt, nreal, dest3, h, zero_rows)


def _up_kernel(be_ref, nreal_ref, x_ref, wg_ref, wu_ref, bg_ref, bu_ref, a_ref):
    b = pl.program_id(0)

    @pl.when(b < nreal_ref[0])
    def _():
        x = x_ref[...].astype(BF16)
        g = jnp.dot(x, wg_ref[0], preferred_element_type=F32) + bg_ref[0]
        u = jnp.dot(x, wu_ref[0], preferred_element_type=F32) + bu_ref[0]
        g = jnp.minimum(g, SWIGLU_LIMIT)
        u = jnp.clip(u, -SWIGLU_LIMIT, SWIGLU_LIMIT)
        a_ref[...] = (g * jax.nn.sigmoid(SWIGLU_ALPHA * g) * (u + 1.0)).astype(a_ref.dtype)

    @pl.when(b >= nreal_ref[0])
    def _():
        a_ref[...] = jnp.zeros_like(a_ref)


def _down_kernel(be_ref, nreal_ref, a_ref, w_ref, b_ref, y_ref, wbf_ref):
    b = pl.program_id(0)

    @pl.when((b == 0) | (be_ref[b] != be_ref[jnp.maximum(b - 1, 0)]))
    def _():
        wbf_ref[...] = w_ref[0].astype(BF16)

    @pl.when(b < nreal_ref[0])
    def _():
        y_ref[...] = jnp.dot(a_ref[...], wbf_ref[...], preferred_element_type=F32) + b_ref[0]

    @pl.when(b >= nreal_ref[0])
    def _():
        y_ref[...] = jnp.zeros_like(y_ref)


def _row_block(b, be, nr):
    return (jnp.minimum(b, nr[0] - 1), 0)


def _expert_block(b, be, nr):
    return (be[b], 0, 0)


def _w1_prep_kernel(w_ref, p_ref, g_ref, u_ref):
    half = W1_TILE // 2
    for c in range(W1_TILES_PER_STEP):
        w = w_ref[0, :, c * W1_TILE:(c + 1) * W1_TILE].astype(BF16)
        r = jnp.dot(w, p_ref[...], preferred_element_type=F32)
        g_ref[0, :, c * half:(c + 1) * half] = r[:, :half].astype(BF16)
        u_ref[0, :, c * half:(c + 1) * half] = r[:, half:].astype(BF16)


def _w1_prep(w1):
    half = W1_TILE // 2
    i = jnp.arange(W1_TILE)[:, None]
    c = jnp.arange(W1_TILE)[None, :]
    perm = jnp.where(c < half, i == 2 * c, i == 2 * (c - half) + 1).astype(BF16)
    out = jax.ShapeDtypeStruct((N_EXPERTS, D_MODEL, D_FF), BF16)
    step_cols = W1_TILE * W1_TILES_PER_STEP
    return pl.pallas_call(
        _w1_prep_kernel,
        grid=(N_EXPERTS, 2 * D_FF // step_cols),
        in_specs=[
            pl.BlockSpec((1, D_MODEL, step_cols), lambda e, j: (e, 0, j)),
            pl.BlockSpec((W1_TILE, W1_TILE), lambda e, j: (0, 0)),
        ],
        out_specs=[
            pl.BlockSpec((1, D_MODEL, step_cols // 2), lambda e, j: (e, 0, j)),
            pl.BlockSpec((1, D_MODEL, step_cols // 2), lambda e, j: (e, 0, j)),
        ],
        out_shape=[out, out],
        compiler_params=_cparams(("parallel", "parallel"), 40),
        name="w1_prep",
    )(w1, perm)


def _experts(block_e, nreal, xs, w1g, w1u, b1g, b1u, w2, b2):
    n_rows = xs.shape[0]
    nb = n_rows // EXPERT_BLOCK
    act = pl.pallas_call(
        _up_kernel,
        grid_spec=pltpu.PrefetchScalarGridSpec(
            num_scalar_prefetch=2,
            grid=(nb,),
            in_specs=[
                pl.BlockSpec((EXPERT_BLOCK, D_MODEL), _row_block),
                pl.BlockSpec((1, D_MODEL, D_FF), _expert_block),
                pl.BlockSpec((1, D_MODEL, D_FF), _expert_block),
                pl.BlockSpec((1, 1, D_FF), _expert_block),
                pl.BlockSpec((1, 1, D_FF), _expert_block),
            ],
            out_specs=pl.BlockSpec((EXPERT_BLOCK, D_FF), lambda b, be, nr: (b, 0)),
        ),
        out_shape=jax.ShapeDtypeStruct((n_rows, D_FF), BF16),
        compiler_params=_cparams(("arbitrary",), 56),
        name="expert_up",
    )(block_e, nreal, xs, w1g, w1u, b1g, b1u)
    return pl.pallas_call(
        _down_kernel,
        grid_spec=pltpu.PrefetchScalarGridSpec(
            num_scalar_prefetch=2,
            grid=(nb,),
            in_specs=[
                pl.BlockSpec((EXPERT_BLOCK, D_FF), _row_block),
                pl.BlockSpec((1, D_FF, D_MODEL), _expert_block),
                pl.BlockSpec((1, 1, D_MODEL), _expert_block),
            ],
            out_specs=pl.BlockSpec((EXPERT_BLOCK, D_MODEL), lambda b, be, nr: (b, 0)),
            scratch_shapes=[pltpu.VMEM((D_FF, D_MODEL), BF16)],
        ),
        out_shape=jax.ShapeDtypeStruct((n_rows, D_MODEL), F32),
        compiler_params=_cparams(("arbitrary",), 56),
        name="expert_down",
    )(block_e, nreal, act, w2, b2)


def _combine_kernel(dcur_ref, dnxt_ref, x1_ref, meta_ref, g_ref, y_hbm, o_ref, ybuf, sem):
    i = pl.program_id(0)
    n = pl.num_programs(0)
    tm = TM_COMB
    slot = i % 2

    def row_gather(dref, s, t, k):
        return pltpu.make_async_copy(y_hbm.at[pl.ds(dref[0, 0, TOP_K * t + k], 1)],
                                     ybuf.at[s, k, pl.ds(t, 1)], sem.at[s])

    @pl.when(i == 0)
    def _():
        def body(t, c):
            for k in range(TOP_K):
                row_gather(dcur_ref, 0, t, k).start()
            return c
        lax.fori_loop(0, tm, body, 0, unroll=8)

    for s in range(2):
        @pl.when((i + 1 < n) & (slot == 1 - s))
        def _(s=s):
            for t in range(tm):
                for k in range(TOP_K):
                    row_gather(dnxt_ref, s, t, k).start()

    for k in range(TOP_K):
        pltpu.make_async_copy(y_hbm.at[pl.ds(0, tm)], ybuf.at[slot, k], sem.at[slot]).wait()

    meta = meta_ref[...]
    acc = x1_ref[...]
    for k in range(TOP_K):
        acc = acc + meta[:, META_W + k:META_W + k + 1] * ybuf[slot, k]
    var = jnp.mean(acc * acc, axis=-1, keepdims=True)
    o_ref[...] = (acc * lax.rsqrt(var + RMS_EPS) * g_ref[...]).astype(o_ref.dtype)


def _combine(dest, x1, meta, g, y):
    s = x1.shape[0]
    tm = TM_COMB
    nt = s // tm
    dest3 = dest.reshape(nt, 1, TOP_K * tm)
    return pl.pallas_call(
        _combine_kernel,
        grid=(nt,),
        in_specs=[
            pl.BlockSpec((1, 1, TOP_K * tm), lambda i: (i, 0, 0), memory_space=pltpu.SMEM),
            pl.BlockSpec((1, 1, TOP_K * tm), lambda i: (jnp.minimum(i + 1, nt - 1), 0, 0),
                         memory_space=pltpu.SMEM),
            pl.BlockSpec((tm, D_MODEL), lambda i: (i, 0)),
            pl.BlockSpec((tm, LANES), lambda i: (i, 0)),
            pl.BlockSpec((1, D_MODEL), lambda i: (0, 0)),
            pl.BlockSpec(memory_space=pl.ANY),
        ],
        out_specs=pl.BlockSpec((tm, D_MODEL), lambda i: (i, 0)),
        out_shape=jax.ShapeDtypeStruct((s, D_MODEL), F32),
        scratch_shapes=[
            pltpu.VMEM((2, TOP_K, tm, D_MODEL), F32),
            pltpu.SemaphoreType.DMA((2,)),
        ],
        compiler_params=_cparams(("arbitrary",), 40),
        name="combine",
    )(dest3, dest3, x1, meta, g, y)


def _routing_tables(meta, cnt, n_blocks):
    counts = cnt[0].astype(jnp.int32)
    padded = ((counts + EXPERT_BLOCK - 1) // EXPERT_BLOCK) * EXPERT_BLOCK
    pad_end = jnp.cumsum(padded)
    pad_start = pad_end - padded
    e4 = meta[:, META_E:META_E + TOP_K].astype(jnp.int32)
    pos4 = meta[:, META_POS:META_POS + TOP_K].astype(jnp.int32)
    experts = jnp.arange(N_EXPERTS, dtype=jnp.int32)
    dest = pos4 + jnp.sum(jnp.where(e4[..., None] == experts, pad_start, 0), axis=-1)
    nreal = pad_end[-1:] // EXPERT_BLOCK
    blk = jnp.minimum(jnp.arange(n_blocks, dtype=jnp.int32), nreal[0] - 1)
    block_e = jnp.sum(pad_end[None, :] <= (blk * EXPERT_BLOCK)[:, None], axis=1).astype(jnp.int32)
    return dest, block_e, nreal.astype(jnp.int32), (pad_start + counts).astype(jnp.int32), \
        (padded - counts).astype(jnp.int32)


def kernel(x, norm_mix, w_in, b_gate, rel_bias, w_attn_proj, pool_w, pool_scale, w_pool_proj,
           w_out, norm_ffn, router_w, router_b, w1, b1, w2, b2, norm_final):
    bsz, seq, d = x.shape
    n_tok = bsz * seq
    assert w_in.shape[0] == 1 and bsz == 1 and d == D_MODEL and seq % TM_IN == 0
    n_blocks = -(-(n_tok * TOP_K) // EXPERT_BLOCK) + N_EXPERTS
    n_rows = n_blocks * EXPERT_BLOCK
    x2 = x.reshape(n_tok, d)
    row = lambda v: v.reshape(1, -1).astype(F32)
    l = 0
    proj = _inproj(x2, row(norm_mix[l]), w_in[l].astype(BF16))
    attn = _attention(proj, _attn_bias_table(rel_bias[l]))
    merged = _mix(attn, proj, row(b_gate[l]), w_attn_proj[l].astype(BF16),
                  w_pool_proj[l].astype(BF16), pool_w[l].astype(BF16), row(pool_scale[l]))
    rw = router_w[l].astype(F32)
    rw_hi = rw.astype(BF16)
    rw_lo = (rw - rw_hi.astype(F32)).astype(BF16)
    x1, h, meta, cnt = _post(x2, merged, w_out[l].astype(BF16), row(norm_ffn[l]),
                             rw_hi, rw_lo, row(router_b[l]))
    dest, block_e, nreal, padbase, padcnt = _routing_tables(meta, cnt, n_blocks)
    xs = _dispatch(padbase, padcnt, nreal, dest, h, n_rows)
    w1g, w1u = _w1_prep(w1[l])
    y = _experts(block_e, nreal, xs, w1g, w1u,
                 b1[l][:, None, 0::2].astype(F32), b1[l][:, None, 1::2].astype(F32),
                 w2[l], b2[l][:, None, :].astype(F32))
    out = _combine(dest, x1, meta, row(norm_final), y)
    return out.reshape(bsz, seq, d)
```

```python
import functools

import jax
import jax.numpy as jnp
from jax import lax
from jax.experimental import pallas as pl
from jax.experimental.pallas import tpu as pltpu

D_MODEL = 2048
CHUNK = 64
LEFT_CHUNKS = 8
BAND = (LEFT_CHUNKS + 1) * CHUNK
ATTN_WIDTH = D_MODEL // 2
HEAD_DIM = 64
ATTN_HEADS = ATTN_WIDTH // HEAD_DIM
MAX_REL = 256
POOL_WINDOWS = (2, 4, 8, 16)
POOL_GROUPS = len(POOL_WINDOWS)
POOL_WIDTH = D_MODEL // 2
POOL_GROUP_DIM = POOL_WIDTH // POOL_GROUPS
N_BRANCHES = 2
IN_WIDTH = 3 * ATTN_WIDTH + POOL_WIDTH + N_BRANCHES * D_MODEL
N_EXPERTS = 32
TOP_K = 4
D_FF = D_MODEL
SWIGLU_LIMIT = 7.0
SWIGLU_ALPHA = 1.702
EXPERT_BLOCK = 256
RMS_EPS = 1e-5
NEG_INF = -1e30
LOG2_E = 1.4426950408889634

LANES = 128
MAX_HALO = max(POOL_WINDOWS)

TM_IN, TN_IN = 1024, 1024
Q_GROUP = 4
TQ = Q_GROUP * CHUNK
KV_BLOCKS = LEFT_CHUNKS // Q_GROUP + 1
KV_SPAN = KV_BLOCKS * TQ
TM_MIX = 512
TM_POST = 512
TM_DISP = 1024
DRAIN_ROWS = 512
W1_TILE = 512
W1_TILES_PER_STEP = 2
TM_COMB = 256

MIB = 1024 * 1024
F32 = jnp.float32
BF16 = jnp.bfloat16


def _cparams(sem, vmem_mib):
    return pltpu.CompilerParams(dimension_semantics=sem, vmem_limit_bytes=vmem_mib * MIB)


def _resident(shape, index_map):
    return pl.BlockSpec(shape, index_map, pipeline_mode=pl.Buffered(1))


def _inproj_kernel(x_ref, g_ref, w_ref, o_ref, xn_ref):
    @pl.when(pl.program_id(1) == 0)
    def _():
        x = x_ref[...]
        var = jnp.mean(x * x, axis=-1, keepdims=True)
        xn_ref[...] = (x * lax.rsqrt(var + RMS_EPS) * g_ref[...]).astype(BF16)

    o_ref[...] = jnp.dot(xn_ref[...], w_ref[...], preferred_element_type=F32).astype(o_ref.dtype)


def _inproj(x2, g, w_bf):
    s = x2.shape[0]
    return pl.pallas_call(
        _inproj_kernel,
        grid=(s // TM_IN, IN_WIDTH // TN_IN),
        in_specs=[
            pl.BlockSpec((TM_IN, D_MODEL), lambda i, j: (i, 0)),
            pl.BlockSpec((1, D_MODEL), lambda i, j: (0, 0)),
            pl.BlockSpec((D_MODEL, TN_IN), lambda i, j: (0, j)),
        ],
        out_specs=pl.BlockSpec((TM_IN, TN_IN), lambda i, j: (i, j)),
        out_shape=jax.ShapeDtypeStruct((s, IN_WIDTH), BF16),
        scratch_shapes=[pltpu.VMEM((TM_IN, D_MODEL), BF16)],
        compiler_params=_cparams(("parallel", "arbitrary"), 48),
        name="inproj",
    )(x2, g, w_bf)


def _attn_heads(q_ref, k_refs, v_refs, bias_ref, o_ref, valid):
    lane = lax.broadcasted_iota(jnp.int32, (1, LANES), 1)
    scale = HEAD_DIM ** -0.5 * LOG2_E
    nt = (((1,), (1,)), ((), ()))
    for hp in range(ATTN_HEADS // 2):
        cs = slice(hp * LANES, (hp + 1) * LANES)
        qp = q_ref[:, cs]
        ks = [r[:, cs] for r in k_refs]
        vs = [r[:, cs] for r in v_refs]
        outs = []
        for hh in range(2):
            head_lanes = (lane >= hh * HEAD_DIM) & (lane < (hh + 1) * HEAD_DIM)
            qh = jnp.where(head_lanes, qp, jnp.zeros_like(qp))
            s = jnp.concatenate(
                [lax.dot_general(qh, kb, nt, preferred_element_type=F32) for kb in ks], axis=1)
            s = s * scale + bias_ref[2 * hp + hh]
            if valid is not None:
                s = jnp.where(valid, s, NEG_INF)
            m = jnp.max(s, axis=-1, keepdims=True)
            p = jnp.exp2(s - m)
            l = jnp.sum(p, axis=-1, keepdims=True)
            pb = p.astype(BF16)
            o = jnp.dot(pb[:, 0:TQ], vs[0], preferred_element_type=F32)
            o += jnp.dot(pb[:, TQ:2 * TQ], vs[1], preferred_element_type=F32)
            o += jnp.dot(pb[:, 2 * TQ:3 * TQ], vs[2], preferred_element_type=F32)
            outs.append(o / l)
        o_ref[:, cs] = jnp.where(lane < HEAD_DIM, outs[0], outs[1]).astype(o_ref.dtype)


def _attn_kernel(q_ref, k0_ref, k1_ref, k2_ref, v0_ref, v1_ref, v2_ref, bias_ref, o_ref):
    i = pl.program_id(0)
    k_refs = (k0_ref, k1_ref, k2_ref)
    v_refs = (v0_ref, v1_ref, v2_ref)
    first_full = LEFT_CHUNKS * CHUNK // TQ

    @pl.when(i < first_full)
    def _():
        col = lax.broadcasted_iota(jnp.int32, (1, KV_SPAN), 1)
        _attn_heads(q_ref, k_refs, v_refs, bias_ref, o_ref, col >= (LEFT_CHUNKS * CHUNK - TQ * i))

    @pl.when(i >= first_full)
    def _():
        _attn_heads(q_ref, k_refs, v_refs, bias_ref, o_ref, None)


def _attn_bias_table(rel_bias):
    dist = LEFT_CHUNKS * CHUNK + (CHUNK - 1) - jnp.arange(BAND + CHUNK - 1)
    by_col = rel_bias.astype(F32)[:, jnp.clip(dist, -(CHUNK - 1), MAX_REL) + (CHUNK - 1)]
    band_bias = jnp.stack(
        [by_col[:, CHUNK - 1 - i:CHUNK - 1 - i + BAND] for i in range(CHUNK)], axis=1)
    rows = [
        jnp.pad(band_bias, ((0, 0), (0, 0), (c * CHUNK, KV_SPAN - BAND - c * CHUNK)),
                constant_values=NEG_INF)
        for c in range(Q_GROUP)
    ]
    return jnp.concatenate(rows, axis=1) * LOG2_E


def _attention(proj, bias_tab):
    s = proj.shape[0]
    kcol, vcol = 1, 2

    def kv_spec(back, colblk):
        return pl.BlockSpec((TQ, ATTN_WIDTH), lambda i: (jnp.maximum(i - back, 0), colblk))

    return pl.pallas_call(
        _attn_kernel,
        grid=(s // TQ,),
        in_specs=[
            pl.BlockSpec((TQ, ATTN_WIDTH), lambda i: (i, 0)),
            kv_spec(2, kcol), kv_spec(1, kcol), kv_spec(0, kcol),
            kv_spec(2, vcol), kv_spec(1, vcol), kv_spec(0, vcol),
            _resident((ATTN_HEADS, TQ, KV_SPAN), lambda i: (0, 0, 0)),
        ],
        out_specs=pl.BlockSpec((TQ, ATTN_WIDTH), lambda i: (i, 0)),
        out_shape=jax.ShapeDtypeStruct((s, ATTN_WIDTH), BF16),
        compiler_params=_cparams(("parallel",), 48),
        name="attn",
    )(proj, proj, proj, proj, proj, proj, proj, bias_tab)


def _mix_kernel(attn_ref, pin_ref, gl_ref, bg_ref, wap_ref, wpp_ref, pw_ref, ps_ref,
                o_ref, halo_ref, win_ref, hwin_ref):
    i = pl.program_id(0)
    tm = TM_MIX

    @pl.when(i == 0)
    def _():
        halo_ref[...] = jnp.zeros_like(halo_ref)
        t = lax.broadcasted_iota(jnp.int32, (tm, tm), 0)
        j = lax.broadcasted_iota(jnp.int32, (tm, tm), 1)
        th = lax.broadcasted_iota(jnp.int32, (tm, MAX_HALO), 0)
        jh = lax.broadcasted_iota(jnp.int32, (tm, MAX_HALO), 1)
        for gi, w in enumerate(POOL_WINDOWS):
            win_ref[gi] = jnp.where((t - j >= 0) & (t - j < w), 1.0, 0.0).astype(BF16)
            hwin_ref[gi] = jnp.where(th + MAX_HALO - jh < w, 1.0, 0.0).astype(BF16)

    pos = i * tm + lax.broadcasted_iota(jnp.int32, (tm, 1), 0)
    u_all = pin_ref[...]
    halo = halo_ref[...]
    mixed = []
    for gi, w in enumerate(POOL_WINDOWS):
        cs = slice(gi * POOL_GROUP_DIM, (gi + 1) * POOL_GROUP_DIM)
        u = u_all[:, cs]
        wsum = jnp.dot(win_ref[gi], u, preferred_element_type=F32)
        wsum += jnp.dot(hwin_ref[gi], halo[:, cs], preferred_element_type=F32)
        inv_cnt = jnp.where(pos + 1 >= w, 1.0 / w, 1.0 / jnp.minimum(pos + 1, w).astype(F32))
        pooled = wsum * inv_cnt - u.astype(F32)
        mixed.append(jnp.dot(pooled.astype(BF16), pw_ref[gi], preferred_element_type=F32))
    mixed = jnp.concatenate(mixed, axis=1) * ps_ref[...]
    halo_ref[...] = u_all[tm - MAX_HALO:, :]

    y_pool = jnp.dot(mixed.astype(BF16), wpp_ref[...], preferred_element_type=F32)
    y_attn = jnp.dot(attn_ref[...], wap_ref[...], preferred_element_type=F32)
    gates = jax.nn.sigmoid(gl_ref[...].astype(F32) + bg_ref[...])
    merged = gates[:, :D_MODEL] * y_attn + gates[:, D_MODEL:] * y_pool
    o_ref[...] = merged.astype(o_ref.dtype)


def _mix(attn, proj, b_gate, wap_bf, wpp_bf, pw_bf, pool_scale):
    s = attn.shape[0]
    tm = TM_MIX
    pin_col = 3 * ATTN_WIDTH // POOL_WIDTH
    gl_col = (3 * ATTN_WIDTH + POOL_WIDTH) // (N_BRANCHES * D_MODEL)
    return pl.pallas_call(
        _mix_kernel,
        grid=(s // tm,),
        in_specs=[
            pl.BlockSpec((tm, ATTN_WIDTH), lambda i: (i, 0)),
            pl.BlockSpec((tm, POOL_WIDTH), lambda i: (i, pin_col)),
            pl.BlockSpec((tm, N_BRANCHES * D_MODEL), lambda i: (i, gl_col)),
            pl.BlockSpec((1, N_BRANCHES * D_MODEL), lambda i: (0, 0)),
            _resident((ATTN_WIDTH, D_MODEL), lambda i: (0, 0)),
            _resident((POOL_WIDTH, D_MODEL), lambda i: (0, 0)),
            _resident((POOL_GROUPS, POOL_GROUP_DIM, POOL_GROUP_DIM), lambda i: (0, 0, 0)),
            pl.BlockSpec((1, POOL_WIDTH), lambda i: (0, 0)),
        ],
        out_specs=pl.BlockSpec((tm, D_MODEL), lambda i: (i, 0)),
        out_shape=jax.ShapeDtypeStruct((s, D_MODEL), BF16),
        scratch_shapes=[
            pltpu.VMEM((MAX_HALO, POOL_WIDTH), BF16),
            pltpu.VMEM((POOL_GROUPS, tm, tm), BF16),
            pltpu.VMEM((POOL_GROUPS, tm, MAX_HALO), BF16),
        ],
        compiler_params=_cparams(("arbitrary",), 48),
        name="mix",
    )(attn, proj, proj, b_gate, wap_bf, wpp_bf, pw_bf, pool_scale)


META_E, META_POS, META_W = 0, TOP_K, 2 * TOP_K


def _post_kernel(x_ref, m_ref, wout_ref, g_ref, rwh_ref, rwl_ref, rb_ref,
                 x1_ref, h_ref, meta_ref, cnt_ref, carry_ref, tri_ref):
    i = pl.program_id(0)
    tm = TM_POST

    @pl.when(i == 0)
    def _():
        carry_ref[...] = jnp.zeros_like(carry_ref)
        t = lax.broadcasted_iota(jnp.int32, (tm, tm), 0)
        j = lax.broadcasted_iota(jnp.int32, (tm, tm), 1)
        tri_ref[...] = jnp.where(j < t, 1.0, 0.0).astype(BF16)

    x1 = x_ref[...] + jnp.dot(m_ref[...], wout_ref[...], preferred_element_type=F32)
    x1_ref[...] = x1
    var = jnp.mean(x1 * x1, axis=-1, keepdims=True)
    h = x1 * lax.rsqrt(var + RMS_EPS) * g_ref[...]
    h_ref[...] = h

    h_hi = h.astype(BF16)
    h_lo = (h - h_hi.astype(F32)).astype(BF16)
    logits = (jnp.dot(h_hi, rwh_ref[...], preferred_element_type=F32)
              + jnp.dot(h_hi, rwl_ref[...], preferred_element_type=F32)
              + jnp.dot(h_lo, rwh_ref[...], preferred_element_type=F32)
              + rb_ref[...])

    lane_e = lax.broadcasted_iota(jnp.int32, (tm, N_EXPERTS), 1).astype(F32)
    work = logits
    sel = jnp.zeros((tm, N_EXPERTS), F32)
    picks = []
    for _ in range(TOP_K):
        m = jnp.max(work, axis=-1, keepdims=True)
        idx = jnp.min(jnp.where(work == m, lane_e, float(N_EXPERTS)), axis=-1, keepdims=True)
        onehot = lane_e == idx
        picks.append((m, idx, onehot))
        sel = jnp.where(onehot, 1.0, sel)
        work = jnp.where(onehot, -jnp.inf, work)

    rank = jnp.dot(tri_ref[...], sel.astype(BF16), preferred_element_type=F32) + carry_ref[...]
    carry_ref[...] += jnp.sum(sel, axis=0, keepdims=True)
    cnt_ref[...] = carry_ref[...]

    top = picks[0][0]
    exps = [jnp.exp(m - top) for m, _, _ in picks]
    denom = exps[0] + exps[1] + exps[2] + exps[3]
    lane = lax.broadcasted_iota(jnp.int32, (tm, LANES), 1)
    meta = jnp.zeros((tm, LANES), F32)
    for k, (m, idx, onehot) in enumerate(picks):
        pos_k = jnp.sum(jnp.where(onehot, rank, 0.0), axis=-1, keepdims=True)
        meta = jnp.where(lane == META_E + k, idx, meta)
        meta = jnp.where(lane == META_POS + k, pos_k, meta)
        meta = jnp.where(lane == META_W + k, exps[k] / denom, meta)
    meta_ref[...] = meta


def _post(x2, merged, wout_bf, g, rw_hi, rw_lo, rb):
    s = x2.shape[0]
    tm = TM_POST
    return pl.pallas_call(
        _post_kernel,
        grid=(s // tm,),
        in_specs=[
            pl.BlockSpec((tm, D_MODEL), lambda i: (i, 0)),
            pl.BlockSpec((tm, D_MODEL), lambda i: (i, 0)),
            _resident((D_MODEL, D_MODEL), lambda i: (0, 0)),
            pl.BlockSpec((1, D_MODEL), lambda i: (0, 0)),
            pl.BlockSpec((D_MODEL, N_EXPERTS), lambda i: (0, 0)),
            pl.BlockSpec((D_MODEL, N_EXPERTS), lambda i: (0, 0)),
            pl.BlockSpec((1, N_EXPERTS), lambda i: (0, 0)),
        ],
        out_specs=[
            pl.BlockSpec((tm, D_MODEL), lambda i: (i, 0)),
            pl.BlockSpec((tm, D_MODEL), lambda i: (i, 0)),
            pl.BlockSpec((tm, LANES), lambda i: (i, 0)),
            pl.BlockSpec((1, N_EXPERTS), lambda i: (0, 0)),
        ],
        out_shape=[
            jax.ShapeDtypeStruct((s, D_MODEL), F32),
            jax.ShapeDtypeStruct((s, D_MODEL), F32),
            jax.ShapeDtypeStruct((s, LANES), F32),
            jax.ShapeDtypeStruct((1, N_EXPERTS), F32),
        ],
        scratch_shapes=[
            pltpu.VMEM((1, N_EXPERTS), F32),
            pltpu.VMEM((tm, tm), BF16),
        ],
        compiler_params=_cparams(("arbitrary",), 48),
        name="post",
    )(x2, merged, wout_bf, g, rw_hi, rw_lo, rb)


def _row_copy(src, src_row, dst, dst_row, sem):
    return pltpu.make_async_copy(src.at[pl.ds(src_row, 1)], dst.at[pl.ds(dst_row, 1)], sem)


def _block_copy(src, dst, dst_block, sem):
    return pltpu.make_async_copy(src, dst.at[pl.ds(dst_block * EXPERT_BLOCK, EXPERT_BLOCK)], sem)


def _dispatch_kernel(padbase_ref, padcnt_ref, nreal_ref, dest_ref, h_ref, zero_hbm, xs_hbm,
                     sem, pad_sem):
    i = pl.program_id(0)
    tm = TM_DISP
    n_blocks = xs_hbm.shape[0] // EXPERT_BLOCK

    for t in range(tm):
        for k in range(TOP_K):
            _row_copy(h_ref, t, xs_hbm, dest_ref[0, 0, TOP_K * t + k], sem).start(priority=k % 2)

    @pl.when(i == 0)
    def _():
        for e in range(N_EXPERTS):
            def fill(r, c, e=e):
                _row_copy(zero_hbm, 0, xs_hbm, padbase_ref[e] + r, pad_sem).start()
                return c
            lax.fori_loop(0, padcnt_ref[e], fill, 0)

        def fill_block(b, c):
            _block_copy(zero_hbm, xs_hbm, b, pad_sem).start()
            return c
        lax.fori_loop(nreal_ref[0], n_blocks, fill_block, 0)

        for e in range(N_EXPERTS):
            def drain(r, c):
                _row_copy(zero_hbm, 0, xs_hbm, 0, pad_sem).wait()
                return c
            lax.fori_loop(0, padcnt_ref[e], drain, 0)

        def drain_block(b, c):
            _block_copy(zero_hbm, xs_hbm, 0, pad_sem).wait()
            return c
        lax.fori_loop(nreal_ref[0], n_blocks, drain_block, 0)

    for _ in range(TOP_K * tm // DRAIN_ROWS):
        pltpu.make_async_copy(h_ref.at[pl.ds(0, DRAIN_ROWS)], xs_hbm.at[pl.ds(0, DRAIN_ROWS)],
                              sem).wait()


def _dispatch(padbase, padcnt, nreal, dest, h, n_rows):
    s = h.shape[0]
    tm = TM_DISP
    dest3 = dest.reshape(s // tm, 1, TOP_K * tm)
    zero_rows = jnp.zeros((EXPERT_BLOCK, D_MODEL), F32)
    grid_spec = pltpu.PrefetchScalarGridSpec(
        num_scalar_prefetch=3,
        grid=(s // tm,),
        in_specs=[
            pl.BlockSpec((1, 1, TOP_K * tm), lambda i, pb, pc, nr: (i, 0, 0),
                         memory_space=pltpu.SMEM),
            pl.BlockSpec((tm, D_MODEL), lambda i, pb, pc, nr: (i, 0)),
            pl.BlockSpec(memory_space=pl.ANY),
        ],
        out_specs=pl.BlockSpec(memory_space=pl.ANY),
        scratch_shapes=[pltpu.SemaphoreType.DMA, pltpu.SemaphoreType.DMA],
    )
    return pl.pallas_call(
        _dispatch_kernel,
        grid_spec=grid_spec,
        out_shape=jax.ShapeDtypeStruct((n_rows, D_MODEL), F32),
        compiler_params=_cparams(("arbitrary",), 32),
        name="dispatch",
    )(padbase, padcnt, nreal, dest3, h, zero_rows)


def _up_kernel(be_ref, nreal_ref, x_ref, wg_ref, wu_ref, bg_ref, bu_ref, a_ref):
    b = pl.program_id(0)

    @pl.when(b < nreal_ref[0])
    def _():
        x = x_ref[...].astype(BF16)
        g = jnp.dot(x, wg_ref[0], preferred_element_type=F32) + bg_ref[0]
        u = jnp.dot(x, wu_ref[0], preferred_element_type=F32) + bu_ref[0]
        g = jnp.minimum(g, SWIGLU_LIMIT)
        u = jnp.clip(u, -SWIGLU_LIMIT, SWIGLU_LIMIT)
        a_ref[...] = (g * jax.nn.sigmoid(SWIGLU_ALPHA * g) * (u + 1.0)).astype(a_ref.dtype)

    @pl.when(b >= nreal_ref[0])
    def _():
        a_ref[...] = jnp.zeros_like(a_ref)


def _down_kernel(be_ref, nreal_ref, a_ref, w_ref, b_ref, y_ref, wbf_ref):
    b = pl.program_id(0)

    @pl.when((b == 0) | (be_ref[b] != be_ref[jnp.maximum(b - 1, 0)]))
    def _():
        wbf_ref[...] = w_ref[0].astype(BF16)

    @pl.when(b < nreal_ref[0])
    def _():
        y_ref[...] = jnp.dot(a_ref[...], wbf_ref[...], preferred_element_type=F32) + b_ref[0]

    @pl.when(b >= nreal_ref[0])
    def _():
        y_ref[...] = jnp.zeros_like(y_ref)


def _row_block(b, be, nr):
    return (jnp.minimum(b, nr[0] - 1), 0)


def _expert_block(b, be, nr):
    return (be[b], 0, 0)


def _w1_prep_kernel(w_ref, p_ref, g_ref, u_ref):
    half = W1_TILE // 2
    for c in range(W1_TILES_PER_STEP):
        w = w_ref[0, :, c * W1_TILE:(c + 1) * W1_TILE].astype(BF16)
        r = jnp.dot(w, p_ref[...], preferred_element_type=F32)
        g_ref[0, :, c * half:(c + 1) * half] = r[:, :half].astype(BF16)
        u_ref[0, :, c * half:(c + 1) * half] = r[:, half:].astype(BF16)


def _w1_prep(w1):
    half = W1_TILE // 2
    i = jnp.arange(W1_TILE)[:, None]
    c = jnp.arange(W1_TILE)[None, :]
    perm = jnp.where(c < half, i == 2 * c, i == 2 * (c - half) + 1).astype(BF16)
    out = jax.ShapeDtypeStruct((N_EXPERTS, D_MODEL, D_FF), BF16)
    step_cols = W1_TILE * W1_TILES_PER_STEP
    return pl.pallas_call(
        _w1_prep_kernel,
        grid=(N_EXPERTS, 2 * D_FF // step_cols),
        in_specs=[
            pl.BlockSpec((1, D_MODEL, step_cols), lambda e, j: (e, 0, j)),
            pl.BlockSpec((W1_TILE, W1_TILE), lambda e, j: (0, 0)),
        ],
        out_specs=[
            pl.BlockSpec((1, D_MODEL, step_cols // 2), lambda e, j: (e, 0, j)),
            pl.BlockSpec((1, D_MODEL, step_cols // 2), lambda e, j: (e, 0, j)),
        ],
        out_shape=[out, out],
        compiler_params=_cparams(("parallel", "parallel"), 40),
        name="w1_prep",
    )(w1, perm)


def _experts(block_e, nreal, xs, w1g, w1u, b1g, b1u, w2, b2):
    n_rows = xs.shape[0]
    nb = n_rows // EXPERT_BLOCK
    act = pl.pallas_call(
        _up_kernel,
        grid_spec=pltpu.PrefetchScalarGridSpec(
            num_scalar_prefetch=2,
            grid=(nb,),
            in_specs=[
                pl.BlockSpec((EXPERT_BLOCK, D_MODEL), _row_block),
                pl.BlockSpec((1, D_MODEL, D_FF), _expert_block),
                pl.BlockSpec((1, D_MODEL, D_FF), _expert_block),
                pl.BlockSpec((1, 1, D_FF), _expert_block),
                pl.BlockSpec((1, 1, D_FF), _expert_block),
            ],
            out_specs=pl.BlockSpec((EXPERT_BLOCK, D_FF), lambda b, be, nr: (b, 0)),
        ),
        out_shape=jax.ShapeDtypeStruct((n_rows, D_FF), BF16),
        compiler_params=_cparams(("arbitrary",), 56),
        name="expert_up",
    )(block_e, nreal, xs, w1g, w1u, b1g, b1u)
    return pl.pallas_call(
        _down_kernel,
        grid_spec=pltpu.PrefetchScalarGridSpec(
            num_scalar_prefetch=2,
            grid=(nb,),
            in_specs=[
                pl.BlockSpec((EXPERT_BLOCK, D_FF), _row_block),
                pl.BlockSpec((1, D_FF, D_MODEL), _expert_block),
                pl.BlockSpec((1, 1, D_MODEL), _expert_block),
            ],
            out_specs=pl.BlockSpec((EXPERT_BLOCK, D_MODEL), lambda b, be, nr: (b, 0)),
            scratch_shapes=[pltpu.VMEM((D_FF, D_MODEL), BF16)],
        ),
        out_shape=jax.ShapeDtypeStruct((n_rows, D_MODEL), F32),
        compiler_params=_cparams(("arbitrary",), 56),
        name="expert_down",
    )(block_e, nreal, act, w2, b2)


def _combine_kernel(dcur_ref, dnxt_ref, x1_ref, meta_ref, g_ref, y_hbm, o_ref, ybuf, sem):
    i = pl.program_id(0)
    n = pl.num_programs(0)
    tm = TM_COMB
    slot = i % 2

    def row_gather(dref, s, t, k):
        return pltpu.make_async_copy(y_hbm.at[pl.ds(dref[0, 0, TOP_K * t + k], 1)],
                                     ybuf.at[s, k, pl.ds(t, 1)], sem.at[s])

    @pl.when(i == 0)
    def _():
        def body(t, c):
            for k in range(TOP_K):
                row_gather(dcur_ref, 0, t, k).start()
            return c
        lax.fori_loop(0, tm, body, 0, unroll=8)

    for s in range(2):
        @pl.when((i + 1 < n) & (slot == 1 - s))
        def _(s=s):
            for t in range(tm):
                for k in range(TOP_K):
                    row_gather(dnxt_ref, s, t, k).start(priority=k % 2)

    for k in range(TOP_K):
        pltpu.make_async_copy(y_hbm.at[pl.ds(0, tm)], ybuf.at[slot, k], sem.at[slot]).wait()

    meta = meta_ref[...]
    acc = x1_ref[...]
    for k in range(TOP_K):
        acc = acc + meta[:, META_W + k:META_W + k + 1] * ybuf[slot, k]
    var = jnp.mean(acc * acc, axis=-1, keepdims=True)
    o_ref[...] = (acc * lax.rsqrt(var + RMS_EPS) * g_ref[...]).astype(o_ref.dtype)


def _combine(dest, x1, meta, g, y):
    s = x1.shape[0]
    tm = TM_COMB
    nt = s // tm
    dest3 = dest.reshape(nt, 1, TOP_K * tm)
    return pl.pallas_call(
        _combine_kernel,
        grid=(nt,),
        in_specs=[
            pl.BlockSpec((1, 1, TOP_K * tm), lambda i: (i, 0, 0), memory_space=pltpu.SMEM),
            pl.BlockSpec((1, 1, TOP_K * tm), lambda i: (jnp.minimum(i + 1, nt - 1), 0, 0),
                         memory_space=pltpu.SMEM),
            pl.BlockSpec((tm, D_MODEL), lambda i: (i, 0)),
            pl.BlockSpec((tm, LANES), lambda i: (i, 0)),
            pl.BlockSpec((1, D_MODEL), lambda i: (0, 0)),
            pl.BlockSpec(memory_space=pl.ANY),
        ],
        out_specs=pl.BlockSpec((tm, D_MODEL), lambda i: (i, 0)),
        out_shape=jax.ShapeDtypeStruct((s, D_MODEL), F32),
        scratch_shapes=[
            pltpu.VMEM((2, TOP_K, tm, D_MODEL), F32),
            pltpu.SemaphoreType.DMA((2,)),
        ],
        compiler_params=_cparams(("arbitrary",), 40),
        name="combine",
    )(dest3, dest3, x1, meta, g, y)


def _routing_tables(meta, cnt, n_blocks):
    counts = cnt[0].astype(jnp.int32)
    padded = ((counts + EXPERT_BLOCK - 1) // EXPERT_BLOCK) * EXPERT_BLOCK
    pad_end = jnp.cumsum(padded)
    pad_start = pad_end - padded
    e4 = meta[:, META_E:META_E + TOP_K].astype(jnp.int32)
    pos4 = meta[:, META_POS:META_POS + TOP_K].astype(jnp.int32)
    experts = jnp.arange(N_EXPERTS, dtype=jnp.int32)
    dest = pos4 + jnp.sum(jnp.where(e4[..., None] == experts, pad_start, 0), axis=-1)
    nreal = pad_end[-1:] // EXPERT_BLOCK
    blk = jnp.minimum(jnp.arange(n_blocks, dtype=jnp.int32), nreal[0] - 1)
    block_e = jnp.sum(pad_end[None, :] <= (blk * EXPERT_BLOCK)[:, None], axis=1).astype(jnp.int32)
    return dest, block_e, nreal.astype(jnp.int32), (pad_start + counts).astype(jnp.int32), \
        (padded - counts).astype(jnp.int32)


def kernel(x, norm_mix, w_in, b_gate, rel_bias, w_attn_proj, pool_w, pool_scale, w_pool_proj,
           w_out, norm_ffn, router_w, router_b, w1, b1, w2, b2, norm_final):
    bsz, seq, d = x.shape
    n_tok = bsz * seq
    assert w_in.shape[0] == 1 and bsz == 1 and d == D_MODEL and seq % TM_IN == 0
    n_blocks = -(-(n_tok * TOP_K) // EXPERT_BLOCK) + N_EXPERTS
    n_rows = n_blocks * EXPERT_BLOCK
    x2 = x.reshape(n_tok, d)
    row = lambda v: v.reshape(1, -1).astype(F32)
    l = 0
    proj = _inproj(x2, row(norm_mix[l]), w_in[l].astype(BF16))
    attn = _attention(proj, _attn_bias_table(rel_bias[l]))
    merged = _mix(attn, proj, row(b_gate[l]), w_attn_proj[l].astype(BF16),
                  w_pool_proj[l].astype(BF16), pool_w[l].astype(BF16), row(pool_scale[l]))
    rw = router_w[l].astype(F32)
    rw_hi = rw.astype(BF16)
    rw_lo = (rw - rw_hi.astype(F32)).astype(BF16)
    x1, h, meta, cnt = _post(x2, merged, w_out[l].astype(BF16), row(norm_ffn[l]),
                             rw_hi, rw_lo, row(router_b[l]))
    dest, block_e, nreal, padbase, padcnt = _routing_tables(meta, cnt, n_blocks)
    xs = _dispatch(padbase, padcnt, nreal, dest, h, n_rows)
    w1g, w1u = _w1_prep(w1[l])
    y = _experts(block_e, nreal, xs, w1g, w1u,
                 b1[l][:, None, 0::2].astype(F32), b1[l][:, None, 1::2].astype(F32),
                 w2[l], b2[l][:, None, :].astype(F32))
    out = _combine(dest, x1, meta, row(norm_final), y)
    return out.reshape(bsz, seq, d)
```

```python
import functools

import jax
import jax.numpy as jnp
from jax import lax
from jax.experimental import pallas as pl
from jax.experimental.pallas import tpu as pltpu

D_MODEL = 2048
CHUNK = 64
LEFT_CHUNKS = 8
BAND = (LEFT_CHUNKS + 1) * CHUNK
ATTN_WIDTH = D_MODEL // 2
HEAD_DIM = 64
ATTN_HEADS = ATTN_WIDTH // HEAD_DIM
MAX_REL = 256
POOL_WINDOWS = (2, 4, 8, 16)
POOL_GROUPS = len(POOL_WINDOWS)
POOL_WIDTH = D_MODEL // 2
POOL_GROUP_DIM = POOL_WIDTH // POOL_GROUPS
N_BRANCHES = 2
IN_WIDTH = 3 * ATTN_WIDTH + POOL_WIDTH + N_BRANCHES * D_MODEL
N_EXPERTS = 32
TOP_K = 4
D_FF = D_MODEL
SWIGLU_LIMIT = 7.0
SWIGLU_ALPHA = 1.702
EXPERT_BLOCK = 512
SUB_BLOCK = 256
SUBS_PER_BLOCK = EXPERT_BLOCK // SUB_BLOCK
assert SUBS_PER_BLOCK == 2
UP_COLS = 512
RMS_EPS = 1e-5
NEG_INF = -1e30
LOG2_E = 1.4426950408889634

LANES = 128
MAX_HALO = max(POOL_WINDOWS)

TM_IN, TN_IN = 1024, 1024
Q_GROUP = 4
TQ = Q_GROUP * CHUNK
KV_BLOCKS = LEFT_CHUNKS // Q_GROUP + 1
KV_SPAN = KV_BLOCKS * TQ
TM_MIX = 512
TM_POST = 512
TM_DISP = 1024
DRAIN_ROWS = 512
W1_TILE = 512
W1_TILES_PER_STEP = 2
TM_COMB = 256

MIB = 1024 * 1024
F32 = jnp.float32
BF16 = jnp.bfloat16


def _cparams(sem, vmem_mib):
    return pltpu.CompilerParams(dimension_semantics=sem, vmem_limit_bytes=vmem_mib * MIB)


def _resident(shape, index_map):
    return pl.BlockSpec(shape, index_map, pipeline_mode=pl.Buffered(1))


def _inproj_kernel(x_ref, g_ref, w_ref, o_ref, xn_ref):
    @pl.when(pl.program_id(1) == 0)
    def _():
        x = x_ref[...]
        var = jnp.mean(x * x, axis=-1, keepdims=True)
        xn_ref[...] = (x * lax.rsqrt(var + RMS_EPS) * g_ref[...]).astype(BF16)

    o_ref[...] = jnp.dot(xn_ref[...], w_ref[...], preferred_element_type=F32).astype(o_ref.dtype)


def _inproj(x2, g, w_bf):
    s = x2.shape[0]
    return pl.pallas_call(
        _inproj_kernel,
        grid=(s // TM_IN, IN_WIDTH // TN_IN),
        in_specs=[
            pl.BlockSpec((TM_IN, D_MODEL), lambda i, j: (i, 0)),
            pl.BlockSpec((1, D_MODEL), lambda i, j: (0, 0)),
            pl.BlockSpec((D_MODEL, TN_IN), lambda i, j: (0, j)),
        ],
        out_specs=pl.BlockSpec((TM_IN, TN_IN), lambda i, j: (i, j)),
        out_shape=jax.ShapeDtypeStruct((s, IN_WIDTH), BF16),
        scratch_shapes=[pltpu.VMEM((TM_IN, D_MODEL), BF16)],
        compiler_params=_cparams(("parallel", "arbitrary"), 48),
        name="inproj",
    )(x2, g, w_bf)


def _attn_heads(q_ref, k_refs, v_refs, bias_ref, o_ref, valid):
    lane = lax.broadcasted_iota(jnp.int32, (1, LANES), 1)
    scale = HEAD_DIM ** -0.5 * LOG2_E
    nt = (((1,), (1,)), ((), ()))
    for hp in range(ATTN_HEADS // 2):
        cs = slice(hp * LANES, (hp + 1) * LANES)
        qp = q_ref[:, cs]
        ks = [r[:, cs] for r in k_refs]
        vs = [r[:, cs] for r in v_refs]
        outs = []
        for hh in range(2):
            head_lanes = (lane >= hh * HEAD_DIM) & (lane < (hh + 1) * HEAD_DIM)
            qh = jnp.where(head_lanes, qp, jnp.zeros_like(qp))
            s = jnp.concatenate(
                [lax.dot_general(qh, kb, nt, preferred_element_type=F32) for kb in ks], axis=1)
            s = s * scale + bias_ref[2 * hp + hh]
            if valid is not None:
                s = jnp.where(valid, s, NEG_INF)
            m = jnp.max(s, axis=-1, keepdims=True)
            p = jnp.exp2(s - m)
            l = jnp.sum(p, axis=-1, keepdims=True)
            pb = p.astype(BF16)
            o = jnp.dot(pb[:, 0:TQ], vs[0], preferred_element_type=F32)
            o += jnp.dot(pb[:, TQ:2 * TQ], vs[1], preferred_element_type=F32)
            o += jnp.dot(pb[:, 2 * TQ:3 * TQ], vs[2], preferred_element_type=F32)
            outs.append(o / l)
        o_ref[:, cs] = jnp.where(lane < HEAD_DIM, outs[0], outs[1]).astype(o_ref.dtype)


def _attn_kernel(q_ref, k0_ref, k1_ref, k2_ref, v0_ref, v1_ref, v2_ref, bias_ref, o_ref):
    i = pl.program_id(0)
    k_refs = (k0_ref, k1_ref, k2_ref)
    v_refs = (v0_ref, v1_ref, v2_ref)
    first_full = LEFT_CHUNKS * CHUNK // TQ

    @pl.when(i < first_full)
    def _():
        col = lax.broadcasted_iota(jnp.int32, (1, KV_SPAN), 1)
        _attn_heads(q_ref, k_refs, v_refs, bias_ref, o_ref, col >= (LEFT_CHUNKS * CHUNK - TQ * i))

    @pl.when(i >= first_full)
    def _():
        _attn_heads(q_ref, k_refs, v_refs, bias_ref, o_ref, None)


def _attn_bias_table(rel_bias):
    dist = LEFT_CHUNKS * CHUNK + (CHUNK - 1) - jnp.arange(BAND + CHUNK - 1)
    by_col = rel_bias.astype(F32)[:, jnp.clip(dist, -(CHUNK - 1), MAX_REL) + (CHUNK - 1)]
    band_bias = jnp.stack(
        [by_col[:, CHUNK - 1 - i:CHUNK - 1 - i + BAND] for i in range(CHUNK)], axis=1)
    rows = [
        jnp.pad(band_bias, ((0, 0), (0, 0), (c * CHUNK, KV_SPAN - BAND - c * CHUNK)),
                constant_values=NEG_INF)
        for c in range(Q_GROUP)
    ]
    return jnp.concatenate(rows, axis=1) * LOG2_E


def _attention(proj, bias_tab):
    s = proj.shape[0]
    kcol, vcol = 1, 2

    def kv_spec(back, colblk):
        return pl.BlockSpec((TQ, ATTN_WIDTH), lambda i: (jnp.maximum(i - back, 0), colblk))

    return pl.pallas_call(
        _attn_kernel,
        grid=(s // TQ,),
        in_specs=[
            pl.BlockSpec((TQ, ATTN_WIDTH), lambda i: (i, 0)),
            kv_spec(2, kcol), kv_spec(1, kcol), kv_spec(0, kcol),
            kv_spec(2, vcol), kv_spec(1, vcol), kv_spec(0, vcol),
            _resident((ATTN_HEADS, TQ, KV_SPAN), lambda i: (0, 0, 0)),
        ],
        out_specs=pl.BlockSpec((TQ, ATTN_WIDTH), lambda i: (i, 0)),
        out_shape=jax.ShapeDtypeStruct((s, ATTN_WIDTH), BF16),
        compiler_params=_cparams(("parallel",), 48),
        name="attn",
    )(proj, proj, proj, proj, proj, proj, proj, bias_tab)


def _mix_kernel(attn_ref, pin_ref, gl_ref, bg_ref, wap_ref, wpp_ref, pw_ref, ps_ref,
                o_ref, halo_ref, win_ref, hwin_ref):
    i = pl.program_id(0)
    tm = TM_MIX

    @pl.when(i == 0)
    def _():
        halo_ref[...] = jnp.zeros_like(halo_ref)
        t = lax.broadcasted_iota(jnp.int32, (tm, tm), 0)
        j = lax.broadcasted_iota(jnp.int32, (tm, tm), 1)
        th = lax.broadcasted_iota(jnp.int32, (tm, MAX_HALO), 0)
        jh = lax.broadcasted_iota(jnp.int32, (tm, MAX_HALO), 1)
        for gi, w in enumerate(POOL_WINDOWS):
            win_ref[gi] = jnp.where((t - j >= 0) & (t - j < w), 1.0, 0.0).astype(BF16)
            hwin_ref[gi] = jnp.where(th + MAX_HALO - jh < w, 1.0, 0.0).astype(BF16)

    pos = i * tm + lax.broadcasted_iota(jnp.int32, (tm, 1), 0)
    u_all = pin_ref[...]
    halo = halo_ref[...]
    mixed = []
    for gi, w in enumerate(POOL_WINDOWS):
        cs = slice(gi * POOL_GROUP_DIM, (gi + 1) * POOL_GROUP_DIM)
        u = u_all[:, cs]
        wsum = jnp.dot(win_ref[gi], u, preferred_element_type=F32)
        wsum += jnp.dot(hwin_ref[gi], halo[:, cs], preferred_element_type=F32)
        inv_cnt = jnp.where(pos + 1 >= w, 1.0 / w, 1.0 / jnp.minimum(pos + 1, w).astype(F32))
        pooled = wsum * inv_cnt - u.astype(F32)
        mixed.append(jnp.dot(pooled.astype(BF16), pw_ref[gi], preferred_element_type=F32))
    mixed = jnp.concatenate(mixed, axis=1) * ps_ref[...]
    halo_ref[...] = u_all[tm - MAX_HALO:, :]

    y_pool = jnp.dot(mixed.astype(BF16), wpp_ref[...], preferred_element_type=F32)
    y_attn = jnp.dot(attn_ref[...], wap_ref[...], preferred_element_type=F32)
    gates = jax.nn.sigmoid(gl_ref[...].astype(F32) + bg_ref[...])
    merged = gates[:, :D_MODEL] * y_attn + gates[:, D_MODEL:] * y_pool
    o_ref[...] = merged.astype(o_ref.dtype)


def _mix(attn, proj, b_gate, wap_bf, wpp_bf, pw_bf, pool_scale):
    s = attn.shape[0]
    tm = TM_MIX
    pin_col = 3 * ATTN_WIDTH // POOL_WIDTH
    gl_col = (3 * ATTN_WIDTH + POOL_WIDTH) // (N_BRANCHES * D_MODEL)
    return pl.pallas_call(
        _mix_kernel,
        grid=(s // tm,),
        in_specs=[
            pl.BlockSpec((tm, ATTN_WIDTH), lambda i: (i, 0)),
            pl.BlockSpec((tm, POOL_WIDTH), lambda i: (i, pin_col)),
            pl.BlockSpec((tm, N_BRANCHES * D_MODEL), lambda i: (i, gl_col)),
            pl.BlockSpec((1, N_BRANCHES * D_MODEL), lambda i: (0, 0)),
            _resident((ATTN_WIDTH, D_MODEL), lambda i: (0, 0)),
            _resident((POOL_WIDTH, D_MODEL), lambda i: (0, 0)),
            _resident((POOL_GROUPS, POOL_GROUP_DIM, POOL_GROUP_DIM), lambda i: (0, 0, 0)),
            pl.BlockSpec((1, POOL_WIDTH), lambda i: (0, 0)),
        ],
        out_specs=pl.BlockSpec((tm, D_MODEL), lambda i: (i, 0)),
        out_shape=jax.ShapeDtypeStruct((s, D_MODEL), BF16),
        scratch_shapes=[
            pltpu.VMEM((MAX_HALO, POOL_WIDTH), BF16),
            pltpu.VMEM((POOL_GROUPS, tm, tm), BF16),
            pltpu.VMEM((POOL_GROUPS, tm, MAX_HALO), BF16),
        ],
        compiler_params=_cparams(("arbitrary",), 48),
        name="mix",
    )(attn, proj, proj, b_gate, wap_bf, wpp_bf, pw_bf, pool_scale)


META_E, META_POS, META_W = 0, TOP_K, 2 * TOP_K


def _post_kernel(x_ref, m_ref, wout_ref, g_ref, rwh_ref, rwl_ref, rb_ref,
                 x1_ref, h_ref, meta_ref, cnt_ref, carry_ref, tri_ref):
    i = pl.program_id(0)
    tm = TM_POST

    @pl.when(i == 0)
    def _():
        carry_ref[...] = jnp.zeros_like(carry_ref)
        t = lax.broadcasted_iota(jnp.int32, (tm, tm), 0)
        j = lax.broadcasted_iota(jnp.int32, (tm, tm), 1)
        tri_ref[...] = jnp.where(j < t, 1.0, 0.0).astype(BF16)

    x1 = x_ref[...] + jnp.dot(m_ref[...], wout_ref[...], preferred_element_type=F32)
    x1_ref[...] = x1
    var = jnp.mean(x1 * x1, axis=-1, keepdims=True)
    h = x1 * lax.rsqrt(var + RMS_EPS) * g_ref[...]
    h_ref[...] = h

    h_hi = h.astype(BF16)
    h_lo = (h - h_hi.astype(F32)).astype(BF16)
    logits = (jnp.dot(h_hi, rwh_ref[...], preferred_element_type=F32)
              + jnp.dot(h_hi, rwl_ref[...], preferred_element_type=F32)
              + jnp.dot(h_lo, rwh_ref[...], preferred_element_type=F32)
              + rb_ref[...])

    lane_e = lax.broadcasted_iota(jnp.int32, (tm, N_EXPERTS), 1).astype(F32)
    work = logits
    sel = jnp.zeros((tm, N_EXPERTS), F32)
    picks = []
    for _ in range(TOP_K):
        m = jnp.max(work, axis=-1, keepdims=True)
        idx = jnp.min(jnp.where(work == m, lane_e, float(N_EXPERTS)), axis=-1, keepdims=True)
        onehot = lane_e == idx
        picks.append((m, idx, onehot))
        sel = jnp.where(onehot, 1.0, sel)
        work = jnp.where(onehot, -jnp.inf, work)

    rank = jnp.dot(tri_ref[...], sel.astype(BF16), preferred_element_type=F32) + carry_ref[...]
    carry_ref[...] += jnp.sum(sel, axis=0, keepdims=True)
    cnt_ref[...] = carry_ref[...]

    top = picks[0][0]
    exps = [jnp.exp(m - top) for m, _, _ in picks]
    denom = exps[0] + exps[1] + exps[2] + exps[3]
    lane = lax.broadcasted_iota(jnp.int32, (tm, LANES), 1)
    meta = jnp.zeros((tm, LANES), F32)
    for k, (m, idx, onehot) in enumerate(picks):
        pos_k = jnp.sum(jnp.where(onehot, rank, 0.0), axis=-1, keepdims=True)
        meta = jnp.where(lane == META_E + k, idx, meta)
        meta = jnp.where(lane == META_POS + k, pos_k, meta)
        meta = jnp.where(lane == META_W + k, exps[k] / denom, meta)
    meta_ref[...] = meta


def _post(x2, merged, wout_bf, g, rw_hi, rw_lo, rb):
    s = x2.shape[0]
    tm = TM_POST
    return pl.pallas_call(
        _post_kernel,
        grid=(s // tm,),
        in_specs=[
            pl.BlockSpec((tm, D_MODEL), lambda i: (i, 0)),
            pl.BlockSpec((tm, D_MODEL), lambda i: (i, 0)),
            _resident((D_MODEL, D_MODEL), lambda i: (0, 0)),
            pl.BlockSpec((1, D_MODEL), lambda i: (0, 0)),
            pl.BlockSpec((D_MODEL, N_EXPERTS), lambda i: (0, 0)),
            pl.BlockSpec((D_MODEL, N_EXPERTS), lambda i: (0, 0)),
            pl.BlockSpec((1, N_EXPERTS), lambda i: (0, 0)),
        ],
        out_specs=[
            pl.BlockSpec((tm, D_MODEL), lambda i: (i, 0)),
            pl.BlockSpec((tm, D_MODEL), lambda i: (i, 0)),
            pl.BlockSpec((tm, LANES), lambda i: (i, 0)),
            pl.BlockSpec((1, N_EXPERTS), lambda i: (0, 0)),
        ],
        out_shape=[
            jax.ShapeDtypeStruct((s, D_MODEL), F32),
            jax.ShapeDtypeStruct((s, D_MODEL), F32),
            jax.ShapeDtypeStruct((s, LANES), F32),
            jax.ShapeDtypeStruct((1, N_EXPERTS), F32),
        ],
        scratch_shapes=[
            pltpu.VMEM((1, N_EXPERTS), F32),
            pltpu.VMEM((tm, tm), BF16),
        ],
        compiler_params=_cparams(("arbitrary",), 48),
        name="post",
    )(x2, merged, wout_bf, g, rw_hi, rw_lo, rb)


def _row_copy(src, src_row, dst, dst_row, sem):
    return pltpu.make_async_copy(src.at[pl.ds(src_row, 1)], dst.at[pl.ds(dst_row, 1)], sem)


def _rows_copy(src, dst, dst_row, n_rows, sem):
    return pltpu.make_async_copy(src.at[pl.ds(0, n_rows)],
                                 dst.at[pl.ds(pl.multiple_of(dst_row, SUB_BLOCK), n_rows)], sem)


def _dispatch_kernel(padbase_ref, padcnt_ref, subbase_ref, subcnt_ref, nreal_ref,
                     dest_ref, h_ref, zero_hbm, xs_hbm, sem, pad_sem):
    i = pl.program_id(0)
    tm = TM_DISP
    n_blocks = xs_hbm.shape[0] // EXPERT_BLOCK

    for t in range(tm):
        for k in range(TOP_K):
            _row_copy(h_ref, t, xs_hbm, dest_ref[0, 0, TOP_K * t + k], sem).start(priority=k % 2)

    @pl.when(i == 0)
    def _():
        for e in range(N_EXPERTS):
            def fill(r, c, e=e):
                _row_copy(zero_hbm, 0, xs_hbm, padbase_ref[e] + r, pad_sem).start()
                return c
            lax.fori_loop(0, padcnt_ref[e], fill, 0)

            def fill_sub(r, c, e=e):
                _rows_copy(zero_hbm, xs_hbm, subbase_ref[e], SUB_BLOCK, pad_sem).start()
                return c
            lax.fori_loop(0, subcnt_ref[e], fill_sub, 0)

        def fill_block(b, c):
            _rows_copy(zero_hbm, xs_hbm, b * EXPERT_BLOCK, EXPERT_BLOCK, pad_sem).start()
            return c
        lax.fori_loop(nreal_ref[0], n_blocks, fill_block, 0)

        for e in range(N_EXPERTS):
            def drain(r, c):
                _row_copy(zero_hbm, 0, xs_hbm, 0, pad_sem).wait()
                return c
            lax.fori_loop(0, padcnt_ref[e], drain, 0)

            def drain_sub(r, c):
                _rows_copy(zero_hbm, xs_hbm, 0, SUB_BLOCK, pad_sem).wait()
                return c
            lax.fori_loop(0, subcnt_ref[e], drain_sub, 0)

        def drain_block(b, c):
            _rows_copy(zero_hbm, xs_hbm, 0, EXPERT_BLOCK, pad_sem).wait()
            return c
        lax.fori_loop(nreal_ref[0], n_blocks, drain_block, 0)

    for _ in range(TOP_K * tm // DRAIN_ROWS):
        pltpu.make_async_copy(h_ref.at[pl.ds(0, DRAIN_ROWS)], xs_hbm.at[pl.ds(0, DRAIN_ROWS)],
                              sem).wait()


def _dispatch(fills, nreal, dest, h, n_rows):
    s = h.shape[0]
    tm = TM_DISP
    dest3 = dest.reshape(s // tm, 1, TOP_K * tm)
    zero_rows = jnp.zeros((EXPERT_BLOCK, D_MODEL), F32)
    grid_spec = pltpu.PrefetchScalarGridSpec(
        num_scalar_prefetch=5,
        grid=(s // tm,),
        in_specs=[
            pl.BlockSpec((1, 1, TOP_K * tm), lambda i, *_: (i, 0, 0), memory_space=pltpu.SMEM),
            pl.BlockSpec((tm, D_MODEL), lambda i, *_: (i, 0)),
            pl.BlockSpec(memory_space=pl.ANY),
        ],
        out_specs=pl.BlockSpec(memory_space=pl.ANY),
        scratch_shapes=[pltpu.SemaphoreType.DMA, pltpu.SemaphoreType.DMA],
    )
    return pl.pallas_call(
        _dispatch_kernel,
        grid_spec=grid_spec,
        out_shape=jax.ShapeDtypeStruct((n_rows, D_MODEL), F32),
        compiler_params=_cparams(("arbitrary",), 32),
        name="dispatch",
    )(*fills, nreal, dest3, h, zero_rows)


def _per_sub_block(nsub, o_ref, compute):
    @pl.when(nsub == SUBS_PER_BLOCK)
    def _():
        compute(slice(0, EXPERT_BLOCK))

    for n in range(SUBS_PER_BLOCK):
        @pl.when(nsub == n)
        def _(n=n):
            if n:
                compute(slice(0, n * SUB_BLOCK))
            o_ref[n * SUB_BLOCK:, :] = jnp.zeros((EXPERT_BLOCK - n * SUB_BLOCK, o_ref.shape[1]),
                                                 o_ref.dtype)


def _up_kernel(be_ref, nsub_ref, nreal_ref, x_ref, wg_ref, wu_ref, bg_ref, bu_ref, a_ref):
    def swiglu(rows):
        x = x_ref[rows, :].astype(BF16)
        for c in range(D_FF // UP_COLS):
            cs = slice(c * UP_COLS, (c + 1) * UP_COLS)
            g = jnp.dot(x, wg_ref[0, :, cs], preferred_element_type=F32) + bg_ref[0, :, cs]
            u = jnp.dot(x, wu_ref[0, :, cs], preferred_element_type=F32) + bu_ref[0, :, cs]
            g = jnp.minimum(g, SWIGLU_LIMIT)
            u = jnp.clip(u, -SWIGLU_LIMIT, SWIGLU_LIMIT)
            a_ref[rows, cs] = (g * jax.nn.sigmoid(SWIGLU_ALPHA * g) * (u + 1.0)).astype(a_ref.dtype)

    _per_sub_block(nsub_ref[pl.program_id(0)], a_ref, swiglu)


def _down_kernel(be_ref, nsub_ref, nreal_ref, nexte_ref, a_ref, w_hbm, b_ref, y_ref,
                 stage_ref, wbf_ref, sem):
    b = pl.program_id(0)
    e = be_ref[b]

    def fetch(expert):
        return pltpu.make_async_copy(w_hbm.at[expert], stage_ref, sem)

    @pl.when(b == 0)
    def _():
        fetch(e).start()

    @pl.when((b == 0) | (e != be_ref[jnp.maximum(b - 1, 0)]))
    def _():
        fetch(e).wait()
        wbf_ref[...] = stage_ref[...].astype(BF16)

        @pl.when(nexte_ref[b] != e)
        def _():
            fetch(nexte_ref[b]).start()

    def project(rows):
        y_ref[rows, :] = jnp.dot(a_ref[rows, :], wbf_ref[...], preferred_element_type=F32) + b_ref[0]

    _per_sub_block(nsub_ref[b], y_ref, project)


def _row_block(b, be, ns, nr, *_):
    return (jnp.minimum(b, nr[0] - 1), 0)


def _expert_block(b, be, *_):
    return (be[b], 0, 0)


def _w1_prep_kernel(w_ref, p_ref, g_ref, u_ref):
    half = W1_TILE // 2
    for c in range(W1_TILES_PER_STEP):
        w = w_ref[0, :, c * W1_TILE:(c + 1) * W1_TILE].astype(BF16)
        r = jnp.dot(w, p_ref[...], preferred_element_type=F32)
        g_ref[0, :, c * half:(c + 1) * half] = r[:, :half].astype(BF16)
        u_ref[0, :, c * half:(c + 1) * half] = r[:, half:].astype(BF16)


def _w1_prep(w1):
    half = W1_TILE // 2
    i = jnp.arange(W1_TILE)[:, None]
    c = jnp.arange(W1_TILE)[None, :]
    perm = jnp.where(c < half, i == 2 * c, i == 2 * (c - half) + 1).astype(BF16)
    out = jax.ShapeDtypeStruct((N_EXPERTS, D_MODEL, D_FF), BF16)
    step_cols = W1_TILE * W1_TILES_PER_STEP
    return pl.pallas_call(
        _w1_prep_kernel,
        grid=(N_EXPERTS, 2 * D_FF // step_cols),
        in_specs=[
            pl.BlockSpec((1, D_MODEL, step_cols), lambda e, j: (e, 0, j)),
            pl.BlockSpec((W1_TILE, W1_TILE), lambda e, j: (0, 0)),
        ],
        out_specs=[
            pl.BlockSpec((1, D_MODEL, step_cols // 2), lambda e, j: (e, 0, j)),
            pl.BlockSpec((1, D_MODEL, step_cols // 2), lambda e, j: (e, 0, j)),
        ],
        out_shape=[out, out],
        compiler_params=_cparams(("parallel", "parallel"), 40),
        name="w1_prep",
    )(w1, perm)


def _experts(block_e, nsub, nreal, next_e, xs, w1g, w1u, b1g, b1u, w2, b2):
    n_rows = xs.shape[0]
    nb = n_rows // EXPERT_BLOCK
    act = pl.pallas_call(
        _up_kernel,
        grid_spec=pltpu.PrefetchScalarGridSpec(
            num_scalar_prefetch=3,
            grid=(nb,),
            in_specs=[
                pl.BlockSpec((EXPERT_BLOCK, D_MODEL), _row_block),
                pl.BlockSpec((1, D_MODEL, D_FF), _expert_block),
                pl.BlockSpec((1, D_MODEL, D_FF), _expert_block),
                pl.BlockSpec((1, 1, D_FF), _expert_block),
                pl.BlockSpec((1, 1, D_FF), _expert_block),
            ],
            out_specs=pl.BlockSpec((EXPERT_BLOCK, D_FF), lambda b, *_: (b, 0)),
        ),
        out_shape=jax.ShapeDtypeStruct((n_rows, D_FF), BF16),
        compiler_params=_cparams(("arbitrary",), 56),
        name="expert_up",
    )(block_e, nsub, nreal, xs, w1g, w1u, b1g, b1u)
    return pl.pallas_call(
        _down_kernel,
        grid_spec=pltpu.PrefetchScalarGridSpec(
            num_scalar_prefetch=4,
            grid=(nb,),
            in_specs=[
                pl.BlockSpec((EXPERT_BLOCK, D_FF), _row_block),
                pl.BlockSpec(memory_space=pl.ANY),
                pl.BlockSpec((1, 1, D_MODEL), _expert_block),
            ],
            out_specs=pl.BlockSpec((EXPERT_BLOCK, D_MODEL), lambda b, *_: (b, 0)),
            scratch_shapes=[
                pltpu.VMEM((D_FF, D_MODEL), F32),
                pltpu.VMEM((D_FF, D_MODEL), BF16),
                pltpu.SemaphoreType.DMA,
            ],
        ),
        out_shape=jax.ShapeDtypeStruct((n_rows, D_MODEL), F32),
        compiler_params=_cparams(("arbitrary",), 56),
        name="expert_down",
    )(block_e, nsub, nreal, next_e, act, w2, b2)


def _combine_kernel(dcur_ref, dnxt_ref, x1_ref, meta_ref, g_ref, y_hbm, o_ref, ybuf, sem):
    i = pl.program_id(0)
    n = pl.num_programs(0)
    tm = TM_COMB
    slot = i % 2

    def row_gather(dref, s, t, k):
        return pltpu.make_async_copy(y_hbm.at[pl.ds(dref[0, 0, TOP_K * t + k], 1)],
                                     ybuf.at[s, k, pl.ds(t, 1)], sem.at[s])

    @pl.when(i == 0)
    def _():
        def body(t, c):
            for k in range(TOP_K):
                row_gather(dcur_ref, 0, t, k).start()
            return c
        lax.fori_loop(0, tm, body, 0, unroll=8)

    for s in range(2):
        @pl.when((i + 1 < n) & (slot == 1 - s))
        def _(s=s):
            for t in range(tm):
                for k in range(TOP_K):
                    row_gather(dnxt_ref, s, t, k).start(priority=k % 2)

    for k in range(TOP_K):
        pltpu.make_async_copy(y_hbm.at[pl.ds(0, tm)], ybuf.at[slot, k], sem.at[slot]).wait()

    meta = meta_ref[...]
    acc = x1_ref[...]
    for k in range(TOP_K):
        acc = acc + meta[:, META_W + k:META_W + k + 1] * ybuf[slot, k]
    var = jnp.mean(acc * acc, axis=-1, keepdims=True)
    o_ref[...] = (acc * lax.rsqrt(var + RMS_EPS) * g_ref[...]).astype(o_ref.dtype)


def _combine(dest, x1, meta, g, y):
    s = x1.shape[0]
    tm = TM_COMB
    nt = s // tm
    dest3 = dest.reshape(nt, 1, TOP_K * tm)
    return pl.pallas_call(
        _combine_kernel,
        grid=(nt,),
        in_specs=[
            pl.BlockSpec((1, 1, TOP_K * tm), lambda i: (i, 0, 0), memory_space=pltpu.SMEM),
            pl.BlockSpec((1, 1, TOP_K * tm), lambda i: (jnp.minimum(i + 1, nt - 1), 0, 0),
                         memory_space=pltpu.SMEM),
            pl.BlockSpec((tm, D_MODEL), lambda i: (i, 0)),
            pl.BlockSpec((tm, LANES), lambda i: (i, 0)),
            pl.BlockSpec((1, D_MODEL), lambda i: (0, 0)),
            pl.BlockSpec(memory_space=pl.ANY),
        ],
        out_specs=pl.BlockSpec((tm, D_MODEL), lambda i: (i, 0)),
        out_shape=jax.ShapeDtypeStruct((s, D_MODEL), F32),
        scratch_shapes=[
            pltpu.VMEM((2, TOP_K, tm, D_MODEL), F32),
            pltpu.SemaphoreType.DMA((2,)),
        ],
        compiler_params=_cparams(("arbitrary",), 40),
        name="combine",
    )(dest3, dest3, x1, meta, g, y)


def _routing_tables(meta, cnt, n_blocks):
    counts = cnt[0].astype(jnp.int32)
    padded = ((counts + EXPERT_BLOCK - 1) // EXPERT_BLOCK) * EXPERT_BLOCK
    pad_end = jnp.cumsum(padded)
    pad_start = pad_end - padded
    e4 = meta[:, META_E:META_E + TOP_K].astype(jnp.int32)
    pos4 = meta[:, META_POS:META_POS + TOP_K].astype(jnp.int32)
    experts = jnp.arange(N_EXPERTS, dtype=jnp.int32)
    dest = pos4 + jnp.sum(jnp.where(e4[..., None] == experts, pad_start, 0), axis=-1)
    nreal = pad_end[-1:] // EXPERT_BLOCK
    blk_raw = jnp.arange(n_blocks, dtype=jnp.int32)
    blk = jnp.minimum(blk_raw, nreal[0] - 1)
    block_e = jnp.sum(pad_end[None, :] <= (blk * EXPERT_BLOCK)[:, None], axis=1).astype(jnp.int32)
    onehot_e = block_e[:, None] == experts
    pick = lambda table: jnp.sum(jnp.where(onehot_e, table, 0), axis=1)
    rows_left = pick(counts) - (blk * EXPERT_BLOCK - pick(pad_start))
    nsub = jnp.clip((rows_left + SUB_BLOCK - 1) // SUB_BLOCK, 0, SUBS_PER_BLOCK)
    nsub = jnp.where(blk_raw < nreal[0], nsub, 0)
    later = experts[None, :] > block_e[:, None]
    next_e = jnp.min(jnp.where(later & (counts > 0)[None, :], experts, N_EXPERTS), axis=1)
    next_e = jnp.where(next_e == N_EXPERTS, block_e, next_e)
    sub_rows = ((counts + SUB_BLOCK - 1) // SUB_BLOCK) * SUB_BLOCK
    fills = (pad_start + counts, sub_rows - counts, pad_start + sub_rows,
             (padded - sub_rows) // SUB_BLOCK)
    i32 = lambda v: v.astype(jnp.int32)
    return dest, i32(block_e), i32(nsub), i32(nreal), i32(next_e), tuple(i32(f) for f in fills)


def kernel(x, norm_mix, w_in, b_gate, rel_bias, w_attn_proj, pool_w, pool_scale, w_pool_proj,
           w_out, norm_ffn, router_w, router_b, w1, b1, w2, b2, norm_final):
    bsz, seq, d = x.shape
    n_tok = bsz * seq
    assert w_in.shape[0] == 1 and bsz == 1 and d == D_MODEL and seq % TM_IN == 0
    n_blocks = -(-(n_tok * TOP_K) // EXPERT_BLOCK) + N_EXPERTS
    n_rows = n_blocks * EXPERT_BLOCK
    x2 = x.reshape(n_tok, d)
    row = lambda v: v.reshape(1, -1).astype(F32)
    l = 0
    proj = _inproj(x2, row(norm_mix[l]), w_in[l].astype(BF16))
    attn = _attention(proj, _attn_bias_table(rel_bias[l]))
    merged = _mix(attn, proj, row(b_gate[l]), w_attn_proj[l].astype(BF16),
                  w_pool_proj[l].astype(BF16), pool_w[l].astype(BF16), row(pool_scale[l]))
    rw = router_w[l].astype(F32)
    rw_hi = rw.astype(BF16)
    rw_lo = (rw - rw_hi.astype(F32)).astype(BF16)
    x1, h, meta, cnt = _post(x2, merged, w_out[l].astype(BF16), row(norm_ffn[l]),
                             rw_hi, rw_lo, row(router_b[l]))
    dest, block_e, nsub, nreal, next_e, fills = _routing_tables(meta, cnt, n_blocks)
    xs = _dispatch(fills, nreal, dest, h, n_rows)
    w1g, w1u = _w1_prep(w1[l])
    y = _experts(block_e, nsub, nreal, next_e, xs, w1g, w1u,
                 b1[l][:, None, 0::2].astype(F32), b1[l][:, None, 1::2].astype(F32),
                 w2[l], b2[l][:, None, :].astype(F32))
    out = _combine(dest, x1, meta, row(norm_final), y)
    return out.reshape(bsz, seq, d)
```

```python
import functools

import jax
import jax.numpy as jnp
from jax import lax
from jax.experimental import pallas as pl
from jax.experimental.pallas import tpu as pltpu

D_MODEL = 2048
CHUNK = 64
LEFT_CHUNKS = 8
BAND = (LEFT_CHUNKS + 1) * CHUNK
ATTN_WIDTH = D_MODEL // 2
HEAD_DIM = 64
ATTN_HEADS = ATTN_WIDTH // HEAD_DIM
MAX_REL = 256
POOL_WINDOWS = (2, 4, 8, 16)
POOL_GROUPS = len(POOL_WINDOWS)
POOL_WIDTH = D_MODEL // 2
POOL_GROUP_DIM = POOL_WIDTH // POOL_GROUPS
N_BRANCHES = 2
IN_WIDTH = 3 * ATTN_WIDTH + POOL_WIDTH + N_BRANCHES * D_MODEL
N_EXPERTS = 32
TOP_K = 4
D_FF = D_MODEL
SWIGLU_LIMIT = 7.0
SWIGLU_ALPHA = 1.702
EXPERT_BLOCK = 512
SUB_BLOCK = 256
SUBS_PER_BLOCK = EXPERT_BLOCK // SUB_BLOCK
assert SUBS_PER_BLOCK == 2
UP_COLS = 512
RMS_EPS = 1e-5
NEG_INF = -1e30
LOG2_E = 1.4426950408889634

LANES = 128
MAX_HALO = max(POOL_WINDOWS)

TM_IN, TN_IN = 1024, 1024
Q_GROUP = 4
TQ = Q_GROUP * CHUNK
KV_BLOCKS = LEFT_CHUNKS // Q_GROUP + 1
KV_SPAN = KV_BLOCKS * TQ
TM_MIX = 512
TM_POST = 512
TM_DISP = 1024
DRAIN_ROWS = 512
W1_TILE = 512
W1_TILES_PER_STEP = 2
TM_COMB = 256

MIB = 1024 * 1024
F32 = jnp.float32
BF16 = jnp.bfloat16


def _cparams(sem, vmem_mib):
    return pltpu.CompilerParams(dimension_semantics=sem, vmem_limit_bytes=vmem_mib * MIB)


def _resident(shape, index_map):
    return pl.BlockSpec(shape, index_map, pipeline_mode=pl.Buffered(1))


def _inproj_kernel(x_ref, g_ref, w_ref, o_ref, xn_ref):
    @pl.when(pl.program_id(1) == 0)
    def _():
        x = x_ref[...]
        var = jnp.mean(x * x, axis=-1, keepdims=True)
        xn_ref[...] = (x * lax.rsqrt(var + RMS_EPS) * g_ref[...]).astype(BF16)

    o_ref[...] = jnp.dot(xn_ref[...], w_ref[...], preferred_element_type=F32).astype(o_ref.dtype)


def _inproj(x2, g, w_bf):
    s = x2.shape[0]
    return pl.pallas_call(
        _inproj_kernel,
        grid=(s // TM_IN, IN_WIDTH // TN_IN),
        in_specs=[
            pl.BlockSpec((TM_IN, D_MODEL), lambda i, j: (i, 0)),
            pl.BlockSpec((1, D_MODEL), lambda i, j: (0, 0)),
            pl.BlockSpec((D_MODEL, TN_IN), lambda i, j: (0, j)),
        ],
        out_specs=pl.BlockSpec((TM_IN, TN_IN), lambda i, j: (i, j)),
        out_shape=jax.ShapeDtypeStruct((s, IN_WIDTH), BF16),
        scratch_shapes=[pltpu.VMEM((TM_IN, D_MODEL), BF16)],
        compiler_params=_cparams(("parallel", "arbitrary"), 48),
        name="inproj",
    )(x2, g, w_bf)


def _attn_heads(q_ref, k_refs, v_refs, bias_ref, o_ref, valid):
    lane = lax.broadcasted_iota(jnp.int32, (1, LANES), 1)
    scale = HEAD_DIM ** -0.5 * LOG2_E
    nt = (((1,), (1,)), ((), ()))
    for hp in range(ATTN_HEADS // 2):
        cs = slice(hp * LANES, (hp + 1) * LANES)
        qp = q_ref[:, cs]
        ks = [r[:, cs] for r in k_refs]
        vs = [r[:, cs] for r in v_refs]
        zero = jnp.zeros_like(qp)
        q2 = jnp.concatenate([jnp.where(lane < HEAD_DIM, qp, zero),
                              jnp.where(lane >= HEAD_DIM, qp, zero)], axis=0)
        s = jnp.concatenate(
            [lax.dot_general(q2, kb, nt, preferred_element_type=F32) for kb in ks], axis=1)
        s = s * scale + bias_ref[hp]
        if valid is not None:
            s = jnp.where(valid, s, NEG_INF)
        m = jnp.max(s, axis=-1, keepdims=True)
        p = jnp.exp2(s - m)
        l = jnp.sum(p, axis=-1, keepdims=True)
        pb = p.astype(BF16)
        o = jnp.dot(pb[:, 0:TQ], vs[0], preferred_element_type=F32)
        o += jnp.dot(pb[:, TQ:2 * TQ], vs[1], preferred_element_type=F32)
        o += jnp.dot(pb[:, 2 * TQ:3 * TQ], vs[2], preferred_element_type=F32)
        o = o / l
        o_ref[:, cs] = jnp.where(lane < HEAD_DIM, o[:TQ], o[TQ:]).astype(o_ref.dtype)


def _attn_kernel(q_ref, k0_ref, k1_ref, k2_ref, v0_ref, v1_ref, v2_ref, bias_ref, o_ref):
    i = pl.program_id(0)
    k_refs = (k0_ref, k1_ref, k2_ref)
    v_refs = (v0_ref, v1_ref, v2_ref)
    first_full = LEFT_CHUNKS * CHUNK // TQ

    @pl.when(i < first_full)
    def _():
        col = lax.broadcasted_iota(jnp.int32, (1, KV_SPAN), 1)
        _attn_heads(q_ref, k_refs, v_refs, bias_ref, o_ref, col >= (LEFT_CHUNKS * CHUNK - TQ * i))

    @pl.when(i >= first_full)
    def _():
        _attn_heads(q_ref, k_refs, v_refs, bias_ref, o_ref, None)


def _attn_bias_table(rel_bias):
    dist = LEFT_CHUNKS * CHUNK + (CHUNK - 1) - jnp.arange(BAND + CHUNK - 1)
    by_col = rel_bias.astype(F32)[:, jnp.clip(dist, -(CHUNK - 1), MAX_REL) + (CHUNK - 1)]
    band_bias = jnp.stack(
        [by_col[:, CHUNK - 1 - i:CHUNK - 1 - i + BAND] for i in range(CHUNK)], axis=1)
    rows = [
        jnp.pad(band_bias, ((0, 0), (0, 0), (c * CHUNK, KV_SPAN - BAND - c * CHUNK)),
                constant_values=NEG_INF)
        for c in range(Q_GROUP)
    ]
    table = jnp.concatenate(rows, axis=1) * LOG2_E
    return table.reshape(ATTN_HEADS // 2, 2 * TQ, KV_SPAN)


def _attention(proj, bias_tab):
    s = proj.shape[0]
    kcol, vcol = 1, 2

    def kv_spec(back, colblk):
        return pl.BlockSpec((TQ, ATTN_WIDTH), lambda i: (jnp.maximum(i - back, 0), colblk))

    return pl.pallas_call(
        _attn_kernel,
        grid=(s // TQ,),
        in_specs=[
            pl.BlockSpec((TQ, ATTN_WIDTH), lambda i: (i, 0)),
            kv_spec(2, kcol), kv_spec(1, kcol), kv_spec(0, kcol),
            kv_spec(2, vcol), kv_spec(1, vcol), kv_spec(0, vcol),
            _resident((ATTN_HEADS // 2, 2 * TQ, KV_SPAN), lambda i: (0, 0, 0)),
        ],
        out_specs=pl.BlockSpec((TQ, ATTN_WIDTH), lambda i: (i, 0)),
        out_shape=jax.ShapeDtypeStruct((s, ATTN_WIDTH), BF16),
        compiler_params=_cparams(("parallel",), 48),
        name="attn",
    )(proj, proj, proj, proj, proj, proj, proj, bias_tab)


def _mix_kernel(attn_ref, pin_ref, gl_ref, bg_ref, wap_ref, wpp_ref, pw_ref, ps_ref,
                o_ref, halo_ref, win_ref, hwin_ref):
    i = pl.program_id(0)
    tm = TM_MIX

    @pl.when(i == 0)
    def _():
        halo_ref[...] = jnp.zeros_like(halo_ref)
        t = lax.broadcasted_iota(jnp.int32, (tm, tm), 0)
        j = lax.broadcasted_iota(jnp.int32, (tm, tm), 1)
        th = lax.broadcasted_iota(jnp.int32, (tm, MAX_HALO), 0)
        jh = lax.broadcasted_iota(jnp.int32, (tm, MAX_HALO), 1)
        for gi, w in enumerate(POOL_WINDOWS):
            win_ref[gi] = jnp.where((t - j >= 0) & (t - j < w), 1.0, 0.0).astype(BF16)
            hwin_ref[gi] = jnp.where(th + MAX_HALO - jh < w, 1.0, 0.0).astype(BF16)

    pos = i * tm + lax.broadcasted_iota(jnp.int32, (tm, 1), 0)
    u_all = pin_ref[...]
    halo = halo_ref[...]
    mixed = []
    for gi, w in enumerate(POOL_WINDOWS):
        cs = slice(gi * POOL_GROUP_DIM, (gi + 1) * POOL_GROUP_DIM)
        u = u_all[:, cs]
        wsum = jnp.dot(win_ref[gi], u, preferred_element_type=F32)
        wsum += jnp.dot(hwin_ref[gi], halo[:, cs], preferred_element_type=F32)
        inv_cnt = jnp.where(pos + 1 >= w, 1.0 / w, 1.0 / jnp.minimum(pos + 1, w).astype(F32))
        pooled = wsum * inv_cnt - u.astype(F32)
        mixed.append(jnp.dot(pooled.astype(BF16), pw_ref[gi], preferred_element_type=F32))
    mixed = jnp.concatenate(mixed, axis=1) * ps_ref[...]
    halo_ref[...] = u_all[tm - MAX_HALO:, :]

    y_pool = jnp.dot(mixed.astype(BF16), wpp_ref[...], preferred_element_type=F32)
    y_attn = jnp.dot(attn_ref[...], wap_ref[...], preferred_element_type=F32)
    gates = jax.nn.sigmoid(gl_ref[...].astype(F32) + bg_ref[...])
    merged = gates[:, :D_MODEL] * y_attn + gates[:, D_MODEL:] * y_pool
    o_ref[...] = merged.astype(o_ref.dtype)


def _mix(attn, proj, b_gate, wap_bf, wpp_bf, pw_bf, pool_scale):
    s = attn.shape[0]
    tm = TM_MIX
    pin_col = 3 * ATTN_WIDTH // POOL_WIDTH
    gl_col = (3 * ATTN_WIDTH + POOL_WIDTH) // (N_BRANCHES * D_MODEL)
    return pl.pallas_call(
        _mix_kernel,
        grid=(s // tm,),
        in_specs=[
            pl.BlockSpec((tm, ATTN_WIDTH), lambda i: (i, 0)),
            pl.BlockSpec((tm, POOL_WIDTH), lambda i: (i, pin_col)),
            pl.BlockSpec((tm, N_BRANCHES * D_MODEL), lambda i: (i, gl_col)),
            pl.BlockSpec((1, N_BRANCHES * D_MODEL), lambda i: (0, 0)),
            _resident((ATTN_WIDTH, D_MODEL), lambda i: (0, 0)),
            _resident((POOL_WIDTH, D_MODEL), lambda i: (0, 0)),
            _resident((POOL_GROUPS, POOL_GROUP_DIM, POOL_GROUP_DIM), lambda i: (0, 0, 0)),
            pl.BlockSpec((1, POOL_WIDTH), lambda i: (0, 0)),
        ],
        out_specs=pl.BlockSpec((tm, D_MODEL), lambda i: (i, 0)),
        out_shape=jax.ShapeDtypeStruct((s, D_MODEL), BF16),
        scratch_shapes=[
            pltpu.VMEM((MAX_HALO, POOL_WIDTH), BF16),
            pltpu.VMEM((POOL_GROUPS, tm, tm), BF16),
            pltpu.VMEM((POOL_GROUPS, tm, MAX_HALO), BF16),
        ],
        compiler_params=_cparams(("arbitrary",), 48),
        name="mix",
    )(attn, proj, proj, b_gate, wap_bf, wpp_bf, pw_bf, pool_scale)


META_E, META_POS, META_W = 0, TOP_K, 2 * TOP_K


def _post_kernel(x_ref, m_ref, wout_ref, g_ref, rwh_ref, rwl_ref, rb_ref,
                 x1_ref, h_ref, meta_ref, cnt_ref, carry_ref, tri_ref):
    i = pl.program_id(0)
    tm = TM_POST

    @pl.when(i == 0)
    def _():
        carry_ref[...] = jnp.zeros_like(carry_ref)
        t = lax.broadcasted_iota(jnp.int32, (tm, tm), 0)
        j = lax.broadcasted_iota(jnp.int32, (tm, tm), 1)
        tri_ref[...] = jnp.where(j < t, 1.0, 0.0).astype(BF16)

    x1 = x_ref[...] + jnp.dot(m_ref[...], wout_ref[...], preferred_element_type=F32)
    x1_ref[...] = x1
    var = jnp.mean(x1 * x1, axis=-1, keepdims=True)
    h = x1 * lax.rsqrt(var + RMS_EPS) * g_ref[...]
    h_ref[...] = h

    h_hi = h.astype(BF16)
    h_lo = (h - h_hi.astype(F32)).astype(BF16)
    logits = (jnp.dot(h_hi, rwh_ref[...], preferred_element_type=F32)
              + jnp.dot(h_hi, rwl_ref[...], preferred_element_type=F32)
              + jnp.dot(h_lo, rwh_ref[...], preferred_element_type=F32)
              + rb_ref[...])

    lane_e = lax.broadcasted_iota(jnp.int32, (tm, N_EXPERTS), 1).astype(F32)
    work = logits
    sel = jnp.zeros((tm, N_EXPERTS), F32)
    picks = []
    for _ in range(TOP_K):
        m = jnp.max(work, axis=-1, keepdims=True)
        idx = jnp.min(jnp.where(work == m, lane_e, float(N_EXPERTS)), axis=-1, keepdims=True)
        onehot = lane_e == idx
        picks.append((m, idx, onehot))
        sel = jnp.where(onehot, 1.0, sel)
        work = jnp.where(onehot, -jnp.inf, work)

    rank = jnp.dot(tri_ref[...], sel.astype(BF16), preferred_element_type=F32) + carry_ref[...]
    carry_ref[...] += jnp.sum(sel, axis=0, keepdims=True)
    cnt_ref[...] = carry_ref[...]

    top = picks[0][0]
    exps = [jnp.exp(m - top) for m, _, _ in picks]
    denom = exps[0] + exps[1] + exps[2] + exps[3]
    lane = lax.broadcasted_iota(jnp.int32, (tm, LANES), 1)
    meta = jnp.zeros((tm, LANES), F32)
    for k, (m, idx, onehot) in enumerate(picks):
        pos_k = jnp.sum(jnp.where(onehot, rank, 0.0), axis=-1, keepdims=True)
        meta = jnp.where(lane == META_E + k, idx, meta)
        meta = jnp.where(lane == META_POS + k, pos_k, meta)
        meta = jnp.where(lane == META_W + k, exps[k] / denom, meta)
    meta_ref[...] = meta


def _post(x2, merged, wout_bf, g, rw_hi, rw_lo, rb):
    s = x2.shape[0]
    tm = TM_POST
    return pl.pallas_call(
        _post_kernel,
        grid=(s // tm,),
        in_specs=[
            pl.BlockSpec((tm, D_MODEL), lambda i: (i, 0)),
            pl.BlockSpec((tm, D_MODEL), lambda i: (i, 0)),
            _resident((D_MODEL, D_MODEL), lambda i: (0, 0)),
            pl.BlockSpec((1, D_MODEL), lambda i: (0, 0)),
            pl.BlockSpec((D_MODEL, N_EXPERTS), lambda i: (0, 0)),
            pl.BlockSpec((D_MODEL, N_EXPERTS), lambda i: (0, 0)),
            pl.BlockSpec((1, N_EXPERTS), lambda i: (0, 0)),
        ],
        out_specs=[
            pl.BlockSpec((tm, D_MODEL), lambda i: (i, 0)),
            pl.BlockSpec((tm, D_MODEL), lambda i: (i, 0)),
            pl.BlockSpec((tm, LANES), lambda i: (i, 0)),
            pl.BlockSpec((1, N_EXPERTS), lambda i: (0, 0)),
        ],
        out_shape=[
            jax.ShapeDtypeStruct((s, D_MODEL), F32),
            jax.ShapeDtypeStruct((s, D_MODEL), F32),
            jax.ShapeDtypeStruct((s, LANES), F32),
            jax.ShapeDtypeStruct((1, N_EXPERTS), F32),
        ],
        scratch_shapes=[
            pltpu.VMEM((1, N_EXPERTS), F32),
            pltpu.VMEM((tm, tm), BF16),
        ],
        compiler_params=_cparams(("arbitrary",), 48),
        name="post",
    )(x2, merged, wout_bf, g, rw_hi, rw_lo, rb)


def _row_copy(src, src_row, dst, dst_row, sem):
    return pltpu.make_async_copy(src.at[pl.ds(src_row, 1)], dst.at[pl.ds(dst_row, 1)], sem)


def _rows_copy(src, dst, dst_row, n_rows, sem):
    return pltpu.make_async_copy(src.at[pl.ds(0, n_rows)],
                                 dst.at[pl.ds(pl.multiple_of(dst_row, SUB_BLOCK), n_rows)], sem)


def _dispatch_kernel(padbase_ref, padcnt_ref, subbase_ref, subcnt_ref, nreal_ref,
                     dest_ref, h_ref, zero_hbm, xs_hbm, sem, pad_sem):
    i = pl.program_id(0)
    tm = TM_DISP
    n_blocks = xs_hbm.shape[0] // EXPERT_BLOCK

    for t in range(tm):
        for k in range(TOP_K):
            _row_copy(h_ref, t, xs_hbm, dest_ref[0, 0, TOP_K * t + k], sem).start(priority=k % 2)

    @pl.when(i == 0)
    def _():
        for e in range(N_EXPERTS):
            def fill(r, c, e=e):
                _row_copy(zero_hbm, 0, xs_hbm, padbase_ref[e] + r, pad_sem).start()
                return c
            lax.fori_loop(0, padcnt_ref[e], fill, 0)

            def fill_sub(r, c, e=e):
                _rows_copy(zero_hbm, xs_hbm, subbase_ref[e], SUB_BLOCK, pad_sem).start()
                return c
            lax.fori_loop(0, subcnt_ref[e], fill_sub, 0)

        def fill_block(b, c):
            _rows_copy(zero_hbm, xs_hbm, b * EXPERT_BLOCK, EXPERT_BLOCK, pad_sem).start()
            return c
        lax.fori_loop(nreal_ref[0], n_blocks, fill_block, 0)

        for e in range(N_EXPERTS):
            def drain(r, c):
                _row_copy(zero_hbm, 0, xs_hbm, 0, pad_sem).wait()
                return c
            lax.fori_loop(0, padcnt_ref[e], drain, 0)

            def drain_sub(r, c):
                _rows_copy(zero_hbm, xs_hbm, 0, SUB_BLOCK, pad_sem).wait()
                return c
            lax.fori_loop(0, subcnt_ref[e], drain_sub, 0)

        def drain_block(b, c):
            _rows_copy(zero_hbm, xs_hbm, 0, EXPERT_BLOCK, pad_sem).wait()
            return c
        lax.fori_loop(nreal_ref[0], n_blocks, drain_block, 0)

    for _ in range(TOP_K * tm // DRAIN_ROWS):
        pltpu.make_async_copy(h_ref.at[pl.ds(0, DRAIN_ROWS)], xs_hbm.at[pl.ds(0, DRAIN_ROWS)],
                              sem).wait()


def _dispatch(fills, nreal, dest, h, n_rows):
    s = h.shape[0]
    tm = TM_DISP
    dest3 = dest.reshape(s // tm, 1, TOP_K * tm)
    zero_rows = jnp.zeros((EXPERT_BLOCK, D_MODEL), F32)
    grid_spec = pltpu.PrefetchScalarGridSpec(
        num_scalar_prefetch=5,
        grid=(s // tm,),
        in_specs=[
            pl.BlockSpec((1, 1, TOP_K * tm), lambda i, *_: (i, 0, 0), memory_space=pltpu.SMEM),
            pl.BlockSpec((tm, D_MODEL), lambda i, *_: (i, 0)),
            pl.BlockSpec(memory_space=pl.ANY),
        ],
        out_specs=pl.BlockSpec(memory_space=pl.ANY),
        scratch_shapes=[pltpu.SemaphoreType.DMA, pltpu.SemaphoreType.DMA],
    )
    return pl.pallas_call(
        _dispatch_kernel,
        grid_spec=grid_spec,
        out_shape=jax.ShapeDtypeStruct((n_rows, D_MODEL), F32),
        compiler_params=_cparams(("arbitrary",), 32),
        name="dispatch",
    )(*fills, nreal, dest3, h, zero_rows)


def _per_sub_block(nsub, o_ref, compute):
    @pl.when(nsub == SUBS_PER_BLOCK)
    def _():
        compute(slice(0, EXPERT_BLOCK))

    for n in range(SUBS_PER_BLOCK):
        @pl.when(nsub == n)
        def _(n=n):
            if n:
                compute(slice(0, n * SUB_BLOCK))
            o_ref[n * SUB_BLOCK:, :] = jnp.zeros((EXPERT_BLOCK - n * SUB_BLOCK, o_ref.shape[1]),
                                                 o_ref.dtype)


def _up_kernel(be_ref, nsub_ref, nreal_ref, x_ref, wg_ref, wu_ref, bg_ref, bu_ref, a_ref):
    def swiglu(rows):
        x = x_ref[rows, :].astype(BF16)
        for c in range(D_FF // UP_COLS):
            cs = slice(c * UP_COLS, (c + 1) * UP_COLS)
            g = jnp.dot(x, wg_ref[0, :, cs], preferred_element_type=F32) + bg_ref[0, :, cs]
            u = jnp.dot(x, wu_ref[0, :, cs], preferred_element_type=F32) + bu_ref[0, :, cs]
            g = jnp.minimum(g, SWIGLU_LIMIT)
            u = jnp.clip(u, -SWIGLU_LIMIT, SWIGLU_LIMIT)
            a_ref[rows, cs] = (g * jax.nn.sigmoid(SWIGLU_ALPHA * g) * (u + 1.0)).astype(a_ref.dtype)

    _per_sub_block(nsub_ref[pl.program_id(0)], a_ref, swiglu)


def _down_kernel(be_ref, nsub_ref, nreal_ref, nexte_ref, a_ref, w_hbm, b_ref, y_ref,
                 stage_ref, wbf_ref, sem):
    b = pl.program_id(0)
    e = be_ref[b]

    def fetch(expert):
        return pltpu.make_async_copy(w_hbm.at[expert], stage_ref, sem)

    @pl.when(b == 0)
    def _():
        fetch(e).start()

    @pl.when((b == 0) | (e != be_ref[jnp.maximum(b - 1, 0)]))
    def _():
        fetch(e).wait()
        wbf_ref[...] = stage_ref[...].astype(BF16)

        @pl.when(nexte_ref[b] != e)
        def _():
            fetch(nexte_ref[b]).start()

    def project(rows):
        y_ref[rows, :] = jnp.dot(a_ref[rows, :], wbf_ref[...], preferred_element_type=F32) + b_ref[0]

    _per_sub_block(nsub_ref[b], y_ref, project)


def _row_block(b, be, ns, nr, *_):
    return (jnp.minimum(b, nr[0] - 1), 0)


def _expert_block(b, be, *_):
    return (be[b], 0, 0)


def _w1_prep_kernel(w_ref, p_ref, g_ref, u_ref):
    half = W1_TILE // 2
    for c in range(W1_TILES_PER_STEP):
        w = w_ref[0, :, c * W1_TILE:(c + 1) * W1_TILE].astype(BF16)
        r = jnp.dot(w, p_ref[...], preferred_element_type=F32)
        g_ref[0, :, c * half:(c + 1) * half] = r[:, :half].astype(BF16)
        u_ref[0, :, c * half:(c + 1) * half] = r[:, half:].astype(BF16)


def _w1_prep(w1):
    half = W1_TILE // 2
    i = jnp.arange(W1_TILE)[:, None]
    c = jnp.arange(W1_TILE)[None, :]
    perm = jnp.where(c < half, i == 2 * c, i == 2 * (c - half) + 1).astype(BF16)
    out = jax.ShapeDtypeStruct((N_EXPERTS, D_MODEL, D_FF), BF16)
    step_cols = W1_TILE * W1_TILES_PER_STEP
    return pl.pallas_call(
        _w1_prep_kernel,
        grid=(N_EXPERTS, 2 * D_FF // step_cols),
        in_specs=[
            pl.BlockSpec((1, D_MODEL, step_cols), lambda e, j: (e, 0, j)),
            pl.BlockSpec((W1_TILE, W1_TILE), lambda e, j: (0, 0)),
        ],
        out_specs=[
            pl.BlockSpec((1, D_MODEL, step_cols // 2), lambda e, j: (e, 0, j)),
            pl.BlockSpec((1, D_MODEL, step_cols // 2), lambda e, j: (e, 0, j)),
        ],
        out_shape=[out, out],
        compiler_params=_cparams(("parallel", "parallel"), 40),
        name="w1_prep",
    )(w1, perm)


def _experts(block_e, nsub, nreal, next_e, xs, w1g, w1u, b1g, b1u, w2, b2):
    n_rows = xs.shape[0]
    nb = n_rows // EXPERT_BLOCK
    act = pl.pallas_call(
        _up_kernel,
        grid_spec=pltpu.PrefetchScalarGridSpec(
            num_scalar_prefetch=3,
            grid=(nb,),
            in_specs=[
                pl.BlockSpec((EXPERT_BLOCK, D_MODEL), _row_block),
                pl.BlockSpec((1, D_MODEL, D_FF), _expert_block),
                pl.BlockSpec((1, D_MODEL, D_FF), _expert_block),
                pl.BlockSpec((1, 1, D_FF), _expert_block),
                pl.BlockSpec((1, 1, D_FF), _expert_block),
            ],
            out_specs=pl.BlockSpec((EXPERT_BLOCK, D_FF), lambda b, *_: (b, 0)),
        ),
        out_shape=jax.ShapeDtypeStruct((n_rows, D_FF), BF16),
        compiler_params=_cparams(("arbitrary",), 56),
        name="expert_up",
    )(block_e, nsub, nreal, xs, w1g, w1u, b1g, b1u)
    return pl.pallas_call(
        _down_kernel,
        grid_spec=pltpu.PrefetchScalarGridSpec(
            num_scalar_prefetch=4,
            grid=(nb,),
            in_specs=[
                pl.BlockSpec((EXPERT_BLOCK, D_FF), _row_block),
                pl.BlockSpec(memory_space=pl.ANY),
                pl.BlockSpec((1, 1, D_MODEL), _expert_block),
            ],
            out_specs=pl.BlockSpec((EXPERT_BLOCK, D_MODEL), lambda b, *_: (b, 0)),
            scratch_shapes=[
                pltpu.VMEM((D_FF, D_MODEL), F32),
                pltpu.VMEM((D_FF, D_MODEL), BF16),
                pltpu.SemaphoreType.DMA,
            ],
        ),
        out_shape=jax.ShapeDtypeStruct((n_rows, D_MODEL), F32),
        compiler_params=_cparams(("arbitrary",), 56),
        name="expert_down",
    )(block_e, nsub, nreal, next_e, act, w2, b2)


def _combine_kernel(dcur_ref, dnxt_ref, x1_ref, meta_ref, g_ref, y_hbm, o_ref, ybuf, sem):
    i = pl.program_id(0)
    n = pl.num_programs(0)
    tm = TM_COMB
    slot = i % 2

    def row_gather(dref, s, t, k):
        return pltpu.make_async_copy(y_hbm.at[pl.ds(dref[0, 0, TOP_K * t + k], 1)],
                                     ybuf.at[s, k, pl.ds(t, 1)], sem.at[s])

    @pl.when(i == 0)
    def _():
        def body(t, c):
            for k in range(TOP_K):
                row_gather(dcur_ref, 0, t, k).start()
            return c
        lax.fori_loop(0, tm, body, 0, unroll=8)

    for s in range(2):
        @pl.when((i + 1 < n) & (slot == 1 - s))
        def _(s=s):
            for t in range(tm):
                for k in range(TOP_K):
                    row_gather(dnxt_ref, s, t, k).start(priority=k % 2)

    for k in range(TOP_K):
        pltpu.make_async_copy(y_hbm.at[pl.ds(0, tm)], ybuf.at[slot, k], sem.at[slot]).wait()

    meta = meta_ref[...]
    acc = x1_ref[...]
    for k in range(TOP_K):
        acc = acc + meta[:, META_W + k:META_W + k + 1] * ybuf[slot, k]
    var = jnp.mean(acc * acc, axis=-1, keepdims=True)
    o_ref[...] = (acc * lax.rsqrt(var + RMS_EPS) * g_ref[...]).astype(o_ref.dtype)


def _combine(dest, x1, meta, g, y):
    s = x1.shape[0]
    tm = TM_COMB
    nt = s // tm
    dest3 = dest.reshape(nt, 1, TOP_K * tm)
    return pl.pallas_call(
        _combine_kernel,
        grid=(nt,),
        in_specs=[
            pl.BlockSpec((1, 1, TOP_K * tm), lambda i: (i, 0, 0), memory_space=pltpu.SMEM),
            pl.BlockSpec((1, 1, TOP_K * tm), lambda i: (jnp.minimum(i + 1, nt - 1), 0, 0),
                         memory_space=pltpu.SMEM),
            pl.BlockSpec((tm, D_MODEL), lambda i: (i, 0)),
            pl.BlockSpec((tm, LANES), lambda i: (i, 0)),
            pl.BlockSpec((1, D_MODEL), lambda i: (0, 0)),
            pl.BlockSpec(memory_space=pl.ANY),
        ],
        out_specs=pl.BlockSpec((tm, D_MODEL), lambda i: (i, 0)),
        out_shape=jax.ShapeDtypeStruct((s, D_MODEL), F32),
        scratch_shapes=[
            pltpu.VMEM((2, TOP_K, tm, D_MODEL), F32),
            pltpu.SemaphoreType.DMA((2,)),
        ],
        compiler_params=_cparams(("arbitrary",), 40),
        name="combine",
    )(dest3, dest3, x1, meta, g, y)


def _routing_tables(meta, cnt, n_blocks):
    counts = cnt[0].astype(jnp.int32)
    padded = ((counts + EXPERT_BLOCK - 1) // EXPERT_BLOCK) * EXPERT_BLOCK
    pad_end = jnp.cumsum(padded)
    pad_start = pad_end - padded
    e4 = meta[:, META_E:META_E + TOP_K].astype(jnp.int32)
    pos4 = meta[:, META_POS:META_POS + TOP_K].astype(jnp.int32)
    experts = jnp.arange(N_EXPERTS, dtype=jnp.int32)
    dest = pos4 + jnp.sum(jnp.where(e4[..., None] == experts, pad_start, 0), axis=-1)
    nreal = pad_end[-1:] // EXPERT_BLOCK
    blk_raw = jnp.arange(n_blocks, dtype=jnp.int32)
    blk = jnp.minimum(blk_raw, nreal[0] - 1)
    block_e = jnp.sum(pad_end[None, :] <= (blk * EXPERT_BLOCK)[:, None], axis=1).astype(jnp.int32)
    onehot_e = block_e[:, None] == experts
    pick = lambda table: jnp.sum(jnp.where(onehot_e, table, 0), axis=1)
    rows_left = pick(counts) - (blk * EXPERT_BLOCK - pick(pad_start))
    nsub = jnp.clip((rows_left + SUB_BLOCK - 1) // SUB_BLOCK, 0, SUBS_PER_BLOCK)
    nsub = jnp.where(blk_raw < nreal[0], nsub, 0)
    later = experts[None, :] > block_e[:, None]
    next_e = jnp.min(jnp.where(later & (counts > 0)[None, :], experts, N_EXPERTS), axis=1)
    next_e = jnp.where(next_e == N_EXPERTS, block_e, next_e)
    sub_rows = ((counts + SUB_BLOCK - 1) // SUB_BLOCK) * SUB_BLOCK
    fills = (pad_start + counts, sub_rows - counts, pad_start + sub_rows,
             (padded - sub_rows) // SUB_BLOCK)
    i32 = lambda v: v.astype(jnp.int32)
    return dest, i32(block_e), i32(nsub), i32(nreal), i32(next_e), tuple(i32(f) for f in fills)


def kernel(x, norm_mix, w_in, b_gate, rel_bias, w_attn_proj, pool_w, pool_scale, w_pool_proj,
           w_out, norm_ffn, router_w, router_b, w1, b1, w2, b2, norm_final):
    bsz, seq, d = x.shape
    n_tok = bsz * seq
    assert w_in.shape[0] == 1 and bsz == 1 and d == D_MODEL and seq % TM_IN == 0
    n_blocks = -(-(n_tok * TOP_K) // EXPERT_BLOCK) + N_EXPERTS
    n_rows = n_blocks * EXPERT_BLOCK
    x2 = x.reshape(n_tok, d)
    row = lambda v: v.reshape(1, -1).astype(F32)
    l = 0
    proj = _inproj(x2, row(norm_mix[l]), w_in[l].astype(BF16))
    attn = _attention(proj, _attn_bias_table(rel_bias[l]))
    merged = _mix(attn, proj, row(b_gate[l]), w_attn_proj[l].astype(BF16),
                  w_pool_proj[l].astype(BF16), pool_w[l].astype(BF16), row(pool_scale[l]))
    rw = router_w[l].astype(F32)
    rw_hi = rw.astype(BF16)
    rw_lo = (rw - rw_hi.astype(F32)).astype(BF16)
    x1, h, meta, cnt = _post(x2, merged, w_out[l].astype(BF16), row(norm_ffn[l]),
                             rw_hi, rw_lo, row(router_b[l]))
    dest, block_e, nsub, nreal, next_e, fills = _routing_tables(meta, cnt, n_blocks)
    xs = _dispatch(fills, nreal, dest, h, n_rows)
    w1g, w1u = _w1_prep(w1[l])
    y = _experts(block_e, nsub, nreal, next_e, xs, w1g, w1u,
                 b1[l][:, None, 0::2].astype(F32), b1[l][:, None, 1::2].astype(F32),
                 w2[l], b2[l][:, None, :].astype(F32))
    out = _combine(dest, x1, meta, row(norm_final), y)
    return out.reshape(bsz, seq, d)
```

```python
import functools

import jax
import jax.numpy as jnp
from jax import lax
from jax.experimental import pallas as pl
from jax.experimental.pallas import tpu as pltpu

D_MODEL = 2048
CHUNK = 64
LEFT_CHUNKS = 8
BAND = (LEFT_CHUNKS + 1) * CHUNK
ATTN_WIDTH = D_MODEL // 2
HEAD_DIM = 64
ATTN_HEADS = ATTN_WIDTH // HEAD_DIM
MAX_REL = 256
POOL_WINDOWS = (2, 4, 8, 16)
POOL_GROUPS = len(POOL_WINDOWS)
POOL_WIDTH = D_MODEL // 2
POOL_GROUP_DIM = POOL_WIDTH // POOL_GROUPS
N_BRANCHES = 2
IN_WIDTH = 3 * ATTN_WIDTH + POOL_WIDTH + N_BRANCHES * D_MODEL
N_EXPERTS = 32
TOP_K = 4
D_FF = D_MODEL
SWIGLU_LIMIT = 7.0
SWIGLU_ALPHA = 1.702
EXPERT_BLOCK = 512
SUB_BLOCK = 256
SUBS_PER_BLOCK = EXPERT_BLOCK // SUB_BLOCK
assert SUBS_PER_BLOCK == 2
UP_HALF = D_FF
SPLIT_TILE = 256
UP_COLS = 512
RMS_EPS = 1e-5
NEG_INF = -1e30
LOG2_E = 1.4426950408889634

LANES = 128
MAX_HALO = max(POOL_WINDOWS)

TM_IN, TN_IN = 1024, 1024
Q_GROUP = 4
TQ = Q_GROUP * CHUNK
KV_BLOCKS = LEFT_CHUNKS // Q_GROUP + 1
KV_SPAN = KV_BLOCKS * TQ
TM_MIX = 512
TM_POST = 512
TM_DISP = 1024
DRAIN_ROWS = 512
TM_COMB = 256

MIB = 1024 * 1024
F32 = jnp.float32
BF16 = jnp.bfloat16


def _cparams(sem, vmem_mib):
    return pltpu.CompilerParams(dimension_semantics=sem, vmem_limit_bytes=vmem_mib * MIB)


def _resident(shape, index_map):
    return pl.BlockSpec(shape, index_map, pipeline_mode=pl.Buffered(1))


def _inproj_kernel(x_ref, g_ref, w_ref, o_ref, xn_ref):
    @pl.when(pl.program_id(1) == 0)
    def _():
        x = x_ref[...]
        var = jnp.mean(x * x, axis=-1, keepdims=True)
        xn_ref[...] = (x * lax.rsqrt(var + RMS_EPS) * g_ref[...]).astype(BF16)

    o_ref[...] = jnp.dot(xn_ref[...], w_ref[...], preferred_element_type=F32).astype(o_ref.dtype)


def _inproj(x2, g, w_bf):
    s = x2.shape[0]
    return pl.pallas_call(
        _inproj_kernel,
        grid=(s // TM_IN, IN_WIDTH // TN_IN),
        in_specs=[
            pl.BlockSpec((TM_IN, D_MODEL), lambda i, j: (i, 0)),
            pl.BlockSpec((1, D_MODEL), lambda i, j: (0, 0)),
            pl.BlockSpec((D_MODEL, TN_IN), lambda i, j: (0, j)),
        ],
        out_specs=pl.BlockSpec((TM_IN, TN_IN), lambda i, j: (i, j)),
        out_shape=jax.ShapeDtypeStruct((s, IN_WIDTH), BF16),
        scratch_shapes=[pltpu.VMEM((TM_IN, D_MODEL), BF16)],
        compiler_params=_cparams(("parallel", "arbitrary"), 48),
        name="inproj",
    )(x2, g, w_bf)


def _attn_heads(q_ref, k_refs, v_refs, bias_ref, o_ref, valid):
    lane = lax.broadcasted_iota(jnp.int32, (1, LANES), 1)
    scale = HEAD_DIM ** -0.5 * LOG2_E
    nt = (((1,), (1,)), ((), ()))
    for hp in range(ATTN_HEADS // 2):
        cs = slice(hp * LANES, (hp + 1) * LANES)
        qp = q_ref[:, cs]
        ks = [r[:, cs] for r in k_refs]
        vs = [r[:, cs] for r in v_refs]
        zero = jnp.zeros_like(qp)
        q2 = jnp.concatenate([jnp.where(lane < HEAD_DIM, qp, zero),
                              jnp.where(lane >= HEAD_DIM, qp, zero)], axis=0)
        s = jnp.concatenate(
            [lax.dot_general(q2, kb, nt, preferred_element_type=F32) for kb in ks], axis=1)
        s = s * scale + bias_ref[hp]
        if valid is not None:
            s = jnp.where(valid, s, NEG_INF)
        m = jnp.max(s, axis=-1, keepdims=True)
        p = jnp.exp2(s - m)
        l = jnp.sum(p, axis=-1, keepdims=True)
        pb = p.astype(BF16)
        o = jnp.dot(pb[:, 0:TQ], vs[0], preferred_element_type=F32)
        o += jnp.dot(pb[:, TQ:2 * TQ], vs[1], preferred_element_type=F32)
        o += jnp.dot(pb[:, 2 * TQ:3 * TQ], vs[2], preferred_element_type=F32)
        o = o / l
        o_ref[:, cs] = jnp.where(lane < HEAD_DIM, o[:TQ], o[TQ:]).astype(o_ref.dtype)


def _attn_kernel(q_ref, k0_ref, k1_ref, k2_ref, v0_ref, v1_ref, v2_ref, bias_ref, o_ref):
    i = pl.program_id(0)
    k_refs = (k0_ref, k1_ref, k2_ref)
    v_refs = (v0_ref, v1_ref, v2_ref)
    first_full = LEFT_CHUNKS * CHUNK // TQ

    @pl.when(i < first_full)
    def _():
        col = lax.broadcasted_iota(jnp.int32, (1, KV_SPAN), 1)
        _attn_heads(q_ref, k_refs, v_refs, bias_ref, o_ref, col >= (LEFT_CHUNKS * CHUNK - TQ * i))

    @pl.when(i >= first_full)
    def _():
        _attn_heads(q_ref, k_refs, v_refs, bias_ref, o_ref, None)


def _attn_bias_table(rel_bias):
    dist = LEFT_CHUNKS * CHUNK + (CHUNK - 1) - jnp.arange(BAND + CHUNK - 1)
    by_col = rel_bias.astype(F32)[:, jnp.clip(dist, -(CHUNK - 1), MAX_REL) + (CHUNK - 1)]
    band_bias = jnp.stack(
        [by_col[:, CHUNK - 1 - i:CHUNK - 1 - i + BAND] for i in range(CHUNK)], axis=1)
    rows = [
        jnp.pad(band_bias, ((0, 0), (0, 0), (c * CHUNK, KV_SPAN - BAND - c * CHUNK)),
                constant_values=NEG_INF)
        for c in range(Q_GROUP)
    ]
    table = jnp.concatenate(rows, axis=1) * LOG2_E
    return table.reshape(ATTN_HEADS // 2, 2 * TQ, KV_SPAN)


def _attention(proj, bias_tab):
    s = proj.shape[0]
    kcol, vcol = 1, 2

    def kv_spec(back, colblk):
        return pl.BlockSpec((TQ, ATTN_WIDTH), lambda i: (jnp.maximum(i - back, 0), colblk))

    return pl.pallas_call(
        _attn_kernel,
        grid=(s // TQ,),
        in_specs=[
            pl.BlockSpec((TQ, ATTN_WIDTH), lambda i: (i, 0)),
            kv_spec(2, kcol), kv_spec(1, kcol), kv_spec(0, kcol),
            kv_spec(2, vcol), kv_spec(1, vcol), kv_spec(0, vcol),
            _resident((ATTN_HEADS // 2, 2 * TQ, KV_SPAN), lambda i: (0, 0, 0)),
        ],
        out_specs=pl.BlockSpec((TQ, ATTN_WIDTH), lambda i: (i, 0)),
        out_shape=jax.ShapeDtypeStruct((s, ATTN_WIDTH), BF16),
        compiler_params=_cparams(("parallel",), 48),
        name="attn",
    )(proj, proj, proj, proj, proj, proj, proj, bias_tab)


def _mix_kernel(attn_ref, pin_ref, gl_ref, bg_ref, wap_ref, wpp_ref, pw_ref, ps_ref,
                o_ref, halo_ref, win_ref, hwin_ref):
    i = pl.program_id(0)
    tm = TM_MIX

    @pl.when(i == 0)
    def _():
        halo_ref[...] = jnp.zeros_like(halo_ref)
        t = lax.broadcasted_iota(jnp.int32, (tm, tm), 0)
        j = lax.broadcasted_iota(jnp.int32, (tm, tm), 1)
        th = lax.broadcasted_iota(jnp.int32, (tm, MAX_HALO), 0)
        jh = lax.broadcasted_iota(jnp.int32, (tm, MAX_HALO), 1)
        for gi, w in enumerate(POOL_WINDOWS):
            win_ref[gi] = jnp.where((t - j >= 0) & (t - j < w), 1.0, 0.0).astype(BF16)
            hwin_ref[gi] = jnp.where(th + MAX_HALO - jh < w, 1.0, 0.0).astype(BF16)

    pos = i * tm + lax.broadcasted_iota(jnp.int32, (tm, 1), 0)
    u_all = pin_ref[...]
    halo = halo_ref[...]
    mixed = []
    for gi, w in enumerate(POOL_WINDOWS):
        cs = slice(gi * POOL_GROUP_DIM, (gi + 1) * POOL_GROUP_DIM)
        u = u_all[:, cs]
        wsum = jnp.dot(win_ref[gi], u, preferred_element_type=F32)
        wsum += jnp.dot(hwin_ref[gi], halo[:, cs], preferred_element_type=F32)
        inv_cnt = jnp.where(pos + 1 >= w, 1.0 / w, 1.0 / jnp.minimum(pos + 1, w).astype(F32))
        pooled = wsum * inv_cnt - u.astype(F32)
        mixed.append(jnp.dot(pooled.astype(BF16), pw_ref[gi], preferred_element_type=F32))
    mixed = jnp.concatenate(mixed, axis=1) * ps_ref[...]
    halo_ref[...] = u_all[tm - MAX_HALO:, :]

    y_pool = jnp.dot(mixed.astype(BF16), wpp_ref[...], preferred_element_type=F32)
    y_attn = jnp.dot(attn_ref[...], wap_ref[...], preferred_element_type=F32)
    gates = jax.nn.sigmoid(gl_ref[...].astype(F32) + bg_ref[...])
    merged = gates[:, :D_MODEL] * y_attn + gates[:, D_MODEL:] * y_pool
    o_ref[...] = merged.astype(o_ref.dtype)


def _mix(attn, proj, b_gate, wap_bf, wpp_bf, pw_bf, pool_scale):
    s = attn.shape[0]
    tm = TM_MIX
    pin_col = 3 * ATTN_WIDTH // POOL_WIDTH
    gl_col = (3 * ATTN_WIDTH + POOL_WIDTH) // (N_BRANCHES * D_MODEL)
    return pl.pallas_call(
        _mix_kernel,
        grid=(s // tm,),
        in_specs=[
            pl.BlockSpec((tm, ATTN_WIDTH), lambda i: (i, 0)),
            pl.BlockSpec((tm, POOL_WIDTH), lambda i: (i, pin_col)),
            pl.BlockSpec((tm, N_BRANCHES * D_MODEL), lambda i: (i, gl_col)),
            pl.BlockSpec((1, N_BRANCHES * D_MODEL), lambda i: (0, 0)),
            _resident((ATTN_WIDTH, D_MODEL), lambda i: (0, 0)),
            _resident((POOL_WIDTH, D_MODEL), lambda i: (0, 0)),
            _resident((POOL_GROUPS, POOL_GROUP_DIM, POOL_GROUP_DIM), lambda i: (0, 0, 0)),
            pl.BlockSpec((1, POOL_WIDTH), lambda i: (0, 0)),
        ],
        out_specs=pl.BlockSpec((tm, D_MODEL), lambda i: (i, 0)),
        out_shape=jax.ShapeDtypeStruct((s, D_MODEL), BF16),
        scratch_shapes=[
            pltpu.VMEM((MAX_HALO, POOL_WIDTH), BF16),
            pltpu.VMEM((POOL_GROUPS, tm, tm), BF16),
            pltpu.VMEM((POOL_GROUPS, tm, MAX_HALO), BF16),
        ],
        compiler_params=_cparams(("arbitrary",), 48),
        name="mix",
    )(attn, proj, proj, b_gate, wap_bf, wpp_bf, pw_bf, pool_scale)


META_E, META_POS, META_W = 0, TOP_K, 2 * TOP_K


def _post_kernel(x_ref, m_ref, wout_ref, g_ref, rwh_ref, rwl_ref, rb_ref,
                 x1_ref, h_ref, meta_ref, cnt_ref, carry_ref, tri_ref):
    i = pl.program_id(0)
    tm = TM_POST

    @pl.when(i == 0)
    def _():
        carry_ref[...] = jnp.zeros_like(carry_ref)
        t = lax.broadcasted_iota(jnp.int32, (tm, tm), 0)
        j = lax.broadcasted_iota(jnp.int32, (tm, tm), 1)
        tri_ref[...] = jnp.where(j < t, 1.0, 0.0).astype(BF16)

    x1 = x_ref[...] + jnp.dot(m_ref[...], wout_ref[...], preferred_element_type=F32)
    x1_ref[...] = x1
    var = jnp.mean(x1 * x1, axis=-1, keepdims=True)
    h = x1 * lax.rsqrt(var + RMS_EPS) * g_ref[...]
    h_ref[...] = h

    h_hi = h.astype(BF16)
    h_lo = (h - h_hi.astype(F32)).astype(BF16)
    logits = (jnp.dot(h_hi, rwh_ref[...], preferred_element_type=F32)
              + jnp.dot(h_hi, rwl_ref[...], preferred_element_type=F32)
              + jnp.dot(h_lo, rwh_ref[...], preferred_element_type=F32)
              + rb_ref[...])

    lane_e = lax.broadcasted_iota(jnp.int32, (tm, N_EXPERTS), 1).astype(F32)
    work = logits
    sel = jnp.zeros((tm, N_EXPERTS), F32)
    picks = []
    for _ in range(TOP_K):
        m = jnp.max(work, axis=-1, keepdims=True)
        idx = jnp.min(jnp.where(work == m, lane_e, float(N_EXPERTS)), axis=-1, keepdims=True)
        onehot = lane_e == idx
        picks.append((m, idx, onehot))
        sel = jnp.where(onehot, 1.0, sel)
        work = jnp.where(onehot, -jnp.inf, work)

    rank = jnp.dot(tri_ref[...], sel.astype(BF16), preferred_element_type=F32) + carry_ref[...]
    carry_ref[...] += jnp.sum(sel, axis=0, keepdims=True)
    cnt_ref[...] = carry_ref[...]

    top = picks[0][0]
    exps = [jnp.exp(m - top) for m, _, _ in picks]
    denom = exps[0] + exps[1] + exps[2] + exps[3]
    lane = lax.broadcasted_iota(jnp.int32, (tm, LANES), 1)
    meta = jnp.zeros((tm, LANES), F32)
    for k, (m, idx, onehot) in enumerate(picks):
        pos_k = jnp.sum(jnp.where(onehot, rank, 0.0), axis=-1, keepdims=True)
        meta = jnp.where(lane == META_E + k, idx, meta)
        meta = jnp.where(lane == META_POS + k, pos_k, meta)
        meta = jnp.where(lane == META_W + k, exps[k] / denom, meta)
    meta_ref[...] = meta


def _post(x2, merged, wout_bf, g, rw_hi, rw_lo, rb):
    s = x2.shape[0]
    tm = TM_POST
    return pl.pallas_call(
        _post_kernel,
        grid=(s // tm,),
        in_specs=[
            pl.BlockSpec((tm, D_MODEL), lambda i: (i, 0)),
            pl.BlockSpec((tm, D_MODEL), lambda i: (i, 0)),
            _resident((D_MODEL, D_MODEL), lambda i: (0, 0)),
            pl.BlockSpec((1, D_MODEL), lambda i: (0, 0)),
            pl.BlockSpec((D_MODEL, N_EXPERTS), lambda i: (0, 0)),
            pl.BlockSpec((D_MODEL, N_EXPERTS), lambda i: (0, 0)),
            pl.BlockSpec((1, N_EXPERTS), lambda i: (0, 0)),
        ],
        out_specs=[
            pl.BlockSpec((tm, D_MODEL), lambda i: (i, 0)),
            pl.BlockSpec((tm, D_MODEL), lambda i: (i, 0)),
            pl.BlockSpec((tm, LANES), lambda i: (i, 0)),
            pl.BlockSpec((1, N_EXPERTS), lambda i: (0, 0)),
        ],
        out_shape=[
            jax.ShapeDtypeStruct((s, D_MODEL), F32),
            jax.ShapeDtypeStruct((s, D_MODEL), F32),
            jax.ShapeDtypeStruct((s, LANES), F32),
            jax.ShapeDtypeStruct((1, N_EXPERTS), F32),
        ],
        scratch_shapes=[
            pltpu.VMEM((1, N_EXPERTS), F32),
            pltpu.VMEM((tm, tm), BF16),
        ],
        compiler_params=_cparams(("arbitrary",), 48),
        name="post",
    )(x2, merged, wout_bf, g, rw_hi, rw_lo, rb)


def _row_copy(src, src_row, dst, dst_row, sem):
    return pltpu.make_async_copy(src.at[pl.ds(src_row, 1)], dst.at[pl.ds(dst_row, 1)], sem)


def _rows_copy(src, dst, dst_row, n_rows, sem):
    return pltpu.make_async_copy(src.at[pl.ds(0, n_rows)],
                                 dst.at[pl.ds(pl.multiple_of(dst_row, SUB_BLOCK), n_rows)], sem)


def _dispatch_kernel(padbase_ref, padcnt_ref, subbase_ref, subcnt_ref, nreal_ref,
                     dest_ref, h_ref, zero_hbm, xs_hbm, sem, pad_sem):
    i = pl.program_id(0)
    tm = TM_DISP
    n_blocks = xs_hbm.shape[0] // EXPERT_BLOCK

    for t in range(tm):
        for k in range(TOP_K):
            _row_copy(h_ref, t, xs_hbm, dest_ref[0, 0, TOP_K * t + k], sem).start(priority=k % 2)

    @pl.when(i == 0)
    def _():
        for e in range(N_EXPERTS):
            def fill(r, c, e=e):
                _row_copy(zero_hbm, 0, xs_hbm, padbase_ref[e] + r, pad_sem).start()
                return c
            lax.fori_loop(0, padcnt_ref[e], fill, 0)

            def fill_sub(r, c, e=e):
                _rows_copy(zero_hbm, xs_hbm, subbase_ref[e], SUB_BLOCK, pad_sem).start()
                return c
            lax.fori_loop(0, subcnt_ref[e], fill_sub, 0)

        def fill_block(b, c):
            _rows_copy(zero_hbm, xs_hbm, b * EXPERT_BLOCK, EXPERT_BLOCK, pad_sem).start()
            return c
        lax.fori_loop(nreal_ref[0], n_blocks, fill_block, 0)

        for e in range(N_EXPERTS):
            def drain(r, c):
                _row_copy(zero_hbm, 0, xs_hbm, 0, pad_sem).wait()
                return c
            lax.fori_loop(0, padcnt_ref[e], drain, 0)

            def drain_sub(r, c):
                _rows_copy(zero_hbm, xs_hbm, 0, SUB_BLOCK, pad_sem).wait()
                return c
            lax.fori_loop(0, subcnt_ref[e], drain_sub, 0)

        def drain_block(b, c):
            _rows_copy(zero_hbm, xs_hbm, 0, EXPERT_BLOCK, pad_sem).wait()
            return c
        lax.fori_loop(nreal_ref[0], n_blocks, drain_block, 0)

    for _ in range(TOP_K * tm // DRAIN_ROWS):
        pltpu.make_async_copy(h_ref.at[pl.ds(0, DRAIN_ROWS)], xs_hbm.at[pl.ds(0, DRAIN_ROWS)],
                              sem).wait()


def _dispatch(fills, nreal, dest, h, n_rows):
    s = h.shape[0]
    tm = TM_DISP
    dest3 = dest.reshape(s // tm, 1, TOP_K * tm)
    zero_rows = jnp.zeros((EXPERT_BLOCK, D_MODEL), F32)
    grid_spec = pltpu.PrefetchScalarGridSpec(
        num_scalar_prefetch=5,
        grid=(s // tm,),
        in_specs=[
            pl.BlockSpec((1, 1, TOP_K * tm), lambda i, *_: (i, 0, 0), memory_space=pltpu.SMEM),
            pl.BlockSpec((tm, D_MODEL), lambda i, *_: (i, 0)),
            pl.BlockSpec(memory_space=pl.ANY),
        ],
        out_specs=pl.BlockSpec(memory_space=pl.ANY),
        scratch_shapes=[pltpu.SemaphoreType.DMA, pltpu.SemaphoreType.DMA],
    )
    return pl.pallas_call(
        _dispatch_kernel,
        grid_spec=grid_spec,
        out_shape=jax.ShapeDtypeStruct((n_rows, D_MODEL), F32),
        compiler_params=_cparams(("arbitrary",), 32),
        name="dispatch",
    )(*fills, nreal, dest3, h, zero_rows)


def _per_sub_block(nsub, o_ref, compute):
    @pl.when(nsub == SUBS_PER_BLOCK)
    def _():
        compute(slice(0, EXPERT_BLOCK))

    for n in range(SUBS_PER_BLOCK):
        @pl.when(nsub == n)
        def _(n=n):
            if n:
                compute(slice(0, n * SUB_BLOCK))
            o_ref[n * SUB_BLOCK:, :] = jnp.zeros((EXPERT_BLOCK - n * SUB_BLOCK, o_ref.shape[1]),
                                                 o_ref.dtype)


def _up_kernel(be_ref, nsub_ref, nreal_ref, nexte_ref, x_ref, w_hbm, bg_ref, bu_ref, perm_ref,
               a_ref, stage_ref, wg_ref, wu_ref, sem):
    half = pl.program_id(0)
    b = pl.program_id(1)
    e = be_ref[b]

    def fetch(expert, hf):
        cols = pl.ds(pl.multiple_of(hf * UP_HALF, UP_HALF), UP_HALF)
        return pltpu.make_async_copy(w_hbm.at[expert, :, cols], stage_ref, sem)

    @pl.when((half == 0) & (b == 0))
    def _():
        fetch(e, 0).start()

    @pl.when((b == 0) | (e != be_ref[jnp.maximum(b - 1, 0)]))
    def _():
        fetch(e, half).wait()
        for t in range(UP_HALF // SPLIT_TILE):
            w = stage_ref[:, t * SPLIT_TILE:(t + 1) * SPLIT_TILE].astype(BF16)
            r = jnp.dot(w, perm_ref[...], preferred_element_type=F32)
            cs = slice(t * (SPLIT_TILE // 2), (t + 1) * (SPLIT_TILE // 2))
            wg_ref[:, cs] = r[:, :SPLIT_TILE // 2].astype(BF16)
            wu_ref[:, cs] = r[:, SPLIT_TILE // 2:].astype(BF16)
        last_run = nexte_ref[b] == e

        @pl.when(jnp.logical_not(last_run))
        def _():
            fetch(nexte_ref[b], half).start()

        @pl.when(last_run & (half == 0))
        def _():
            fetch(be_ref[0], 1).start()

    def swiglu(rows):
        x = x_ref[rows, :].astype(BF16)
        for c in range(UP_HALF // 2 // UP_COLS):
            cs = slice(c * UP_COLS, (c + 1) * UP_COLS)
            g = jnp.dot(x, wg_ref[:, cs], preferred_element_type=F32) + bg_ref[0, :, cs]
            u = jnp.dot(x, wu_ref[:, cs], preferred_element_type=F32) + bu_ref[0, :, cs]
            g = jnp.minimum(g, SWIGLU_LIMIT)
            u = jnp.clip(u, -SWIGLU_LIMIT, SWIGLU_LIMIT)
            a_ref[rows, cs] = (g * jax.nn.sigmoid(SWIGLU_ALPHA * g) * (u + 1.0)).astype(a_ref.dtype)

    _per_sub_block(nsub_ref[b], a_ref, swiglu)


def _down_kernel(be_ref, nsub_ref, nreal_ref, nexte_ref, a_ref, w_hbm, b_ref, y_ref,
                 stage_ref, wbf_ref, sem):
    b = pl.program_id(0)
    e = be_ref[b]

    def fetch(expert):
        return pltpu.make_async_copy(w_hbm.at[expert], stage_ref, sem)

    @pl.when(b == 0)
    def _():
        fetch(e).start()

    @pl.when((b == 0) | (e != be_ref[jnp.maximum(b - 1, 0)]))
    def _():
        fetch(e).wait()
        wbf_ref[...] = stage_ref[...].astype(BF16)

        @pl.when(nexte_ref[b] != e)
        def _():
            fetch(nexte_ref[b]).start()

    def project(rows):
        y_ref[rows, :] = jnp.dot(a_ref[rows, :], wbf_ref[...], preferred_element_type=F32) + b_ref[0]

    _per_sub_block(nsub_ref[b], y_ref, project)


def _row_block(b, be, ns, nr, *_):
    return (jnp.minimum(b, nr[0] - 1), 0)


def _expert_block(b, be, *_):
    return (be[b], 0, 0)


def _experts(block_e, nsub, nreal, next_e, xs, w1, b1g, b1u, w2, b2):
    n_rows = xs.shape[0]
    nb = n_rows // EXPERT_BLOCK
    i = jnp.arange(SPLIT_TILE)[:, None]
    c = jnp.arange(SPLIT_TILE)[None, :]
    perm = jnp.where(c < SPLIT_TILE // 2, i == 2 * c, i == 2 * (c - SPLIT_TILE // 2) + 1).astype(BF16)
    act = pl.pallas_call(
        _up_kernel,
        grid_spec=pltpu.PrefetchScalarGridSpec(
            num_scalar_prefetch=4,
            grid=(2 * D_FF // UP_HALF, nb),
            in_specs=[
                pl.BlockSpec((EXPERT_BLOCK, D_MODEL),
                             lambda hf, b, be, ns, nr, ne: (jnp.minimum(b, nr[0] - 1), 0)),
                pl.BlockSpec(memory_space=pl.ANY),
                pl.BlockSpec((1, 1, UP_HALF // 2), lambda hf, b, be, *_: (be[b], 0, hf)),
                pl.BlockSpec((1, 1, UP_HALF // 2), lambda hf, b, be, *_: (be[b], 0, hf)),
                pl.BlockSpec((SPLIT_TILE, SPLIT_TILE), lambda hf, b, *_: (0, 0)),
            ],
            out_specs=pl.BlockSpec((EXPERT_BLOCK, UP_HALF // 2), lambda hf, b, *_: (b, hf)),
            scratch_shapes=[
                pltpu.VMEM((D_MODEL, UP_HALF), F32),
                pltpu.VMEM((D_MODEL, UP_HALF // 2), BF16),
                pltpu.VMEM((D_MODEL, UP_HALF // 2), BF16),
                pltpu.SemaphoreType.DMA,
            ],
        ),
        out_shape=jax.ShapeDtypeStruct((n_rows, D_FF), BF16),
        compiler_params=_cparams(("arbitrary", "arbitrary"), 56),
        name="expert_up",
    )(block_e, nsub, nreal, next_e, xs, w1, b1g, b1u, perm)
    return pl.pallas_call(
        _down_kernel,
        grid_spec=pltpu.PrefetchScalarGridSpec(
            num_scalar_prefetch=4,
            grid=(nb,),
            in_specs=[
                pl.BlockSpec((EXPERT_BLOCK, D_FF), _row_block),
                pl.BlockSpec(memory_space=pl.ANY),
                pl.BlockSpec((1, 1, D_MODEL), _expert_block),
            ],
            out_specs=pl.BlockSpec((EXPERT_BLOCK, D_MODEL), lambda b, *_: (b, 0)),
            scratch_shapes=[
                pltpu.VMEM((D_FF, D_MODEL), F32),
                pltpu.VMEM((D_FF, D_MODEL), BF16),
                pltpu.SemaphoreType.DMA,
            ],
        ),
        out_shape=jax.ShapeDtypeStruct((n_rows, D_MODEL), F32),
        compiler_params=_cparams(("arbitrary",), 56),
        name="expert_down",
    )(block_e, nsub, nreal, next_e, act, w2, b2)


def _combine_kernel(dcur_ref, dnxt_ref, x1_ref, meta_ref, g_ref, y_hbm, o_ref, ybuf, sem):
    i = pl.program_id(0)
    n = pl.num_programs(0)
    tm = TM_COMB
    slot = i % 2

    def row_gather(dref, s, t, k):
        return pltpu.make_async_copy(y_hbm.at[pl.ds(dref[0, 0, TOP_K * t + k], 1)],
                                     ybuf.at[s, k, pl.ds(t, 1)], sem.at[s])

    @pl.when(i == 0)
    def _():
        def body(t, c):
            for k in range(TOP_K):
                row_gather(dcur_ref, 0, t, k).start()
            return c
        lax.fori_loop(0, tm, body, 0, unroll=8)

    for s in range(2):
        @pl.when((i + 1 < n) & (slot == 1 - s))
        def _(s=s):
            for t in range(tm):
                for k in range(TOP_K):
                    row_gather(dnxt_ref, s, t, k).start(priority=k % 2)

    for k in range(TOP_K):
        pltpu.make_async_copy(y_hbm.at[pl.ds(0, tm)], ybuf.at[slot, k], sem.at[slot]).wait()

    meta = meta_ref[...]
    acc = x1_ref[...]
    for k in range(TOP_K):
        acc = acc + meta[:, META_W + k:META_W + k + 1] * ybuf[slot, k]
    var = jnp.mean(acc * acc, axis=-1, keepdims=True)
    o_ref[...] = (acc * lax.rsqrt(var + RMS_EPS) * g_ref[...]).astype(o_ref.dtype)


def _combine(dest, x1, meta, g, y):
    s = x1.shape[0]
    tm = TM_COMB
    nt = s // tm
    dest3 = dest.reshape(nt, 1, TOP_K * tm)
    return pl.pallas_call(
        _combine_kernel,
        grid=(nt,),
        in_specs=[
            pl.BlockSpec((1, 1, TOP_K * tm), lambda i: (i, 0, 0), memory_space=pltpu.SMEM),
            pl.BlockSpec((1, 1, TOP_K * tm), lambda i: (jnp.minimum(i + 1, nt - 1), 0, 0),
                         memory_space=pltpu.SMEM),
            pl.BlockSpec((tm, D_MODEL), lambda i: (i, 0)),
            pl.BlockSpec((tm, LANES), lambda i: (i, 0)),
            pl.BlockSpec((1, D_MODEL), lambda i: (0, 0)),
            pl.BlockSpec(memory_space=pl.ANY),
        ],
        out_specs=pl.BlockSpec((tm, D_MODEL), lambda i: (i, 0)),
        out_shape=jax.ShapeDtypeStruct((s, D_MODEL), F32),
        scratch_shapes=[
            pltpu.VMEM((2, TOP_K, tm, D_MODEL), F32),
            pltpu.SemaphoreType.DMA((2,)),
        ],
        compiler_params=_cparams(("arbitrary",), 40),
        name="combine",
    )(dest3, dest3, x1, meta, g, y)


def _routing_tables(meta, cnt, n_blocks):
    counts = cnt[0].astype(jnp.int32)
    padded = ((counts + EXPERT_BLOCK - 1) // EXPERT_BLOCK) * EXPERT_BLOCK
    pad_end = jnp.cumsum(padded)
    pad_start = pad_end - padded
    e4 = meta[:, META_E:META_E + TOP_K].astype(jnp.int32)
    pos4 = meta[:, META_POS:META_POS + TOP_K].astype(jnp.int32)
    experts = jnp.arange(N_EXPERTS, dtype=jnp.int32)
    dest = pos4 + jnp.sum(jnp.where(e4[..., None] == experts, pad_start, 0), axis=-1)
    nreal = pad_end[-1:] // EXPERT_BLOCK
    blk_raw = jnp.arange(n_blocks, dtype=jnp.int32)
    blk = jnp.minimum(blk_raw, nreal[0] - 1)
    block_e = jnp.sum(pad_end[None, :] <= (blk * EXPERT_BLOCK)[:, None], axis=1).astype(jnp.int32)
    onehot_e = block_e[:, None] == experts
    pick = lambda table: jnp.sum(jnp.where(onehot_e, table, 0), axis=1)
    rows_left = pick(counts) - (blk * EXPERT_BLOCK - pick(pad_start))
    nsub = jnp.clip((rows_left + SUB_BLOCK - 1) // SUB_BLOCK, 0, SUBS_PER_BLOCK)
    nsub = jnp.where(blk_raw < nreal[0], nsub, 0)
    later = experts[None, :] > block_e[:, None]
    next_e = jnp.min(jnp.where(later & (counts > 0)[None, :], experts, N_EXPERTS), axis=1)
    next_e = jnp.where(next_e == N_EXPERTS, block_e, next_e)
    sub_rows = ((counts + SUB_BLOCK - 1) // SUB_BLOCK) * SUB_BLOCK
    fills = (pad_start + counts, sub_rows - counts, pad_start + sub_rows,
             (padded - sub_rows) // SUB_BLOCK)
    i32 = lambda v: v.astype(jnp.int32)
    return dest, i32(block_e), i32(nsub), i32(nreal), i32(next_e), tuple(i32(f) for f in fills)


def kernel(x, norm_mix, w_in, b_gate, rel_bias, w_attn_proj, pool_w, pool_scale, w_pool_proj,
           w_out, norm_ffn, router_w, router_b, w1, b1, w2, b2, norm_final):
    bsz, seq, d = x.shape
    n_tok = bsz * seq
    assert w_in.shape[0] == 1 and bsz == 1 and d == D_MODEL and seq % TM_IN == 0
    n_blocks = -(-(n_tok * TOP_K) // EXPERT_BLOCK) + N_EXPERTS
    n_rows = n_blocks * EXPERT_BLOCK
    x2 = x.reshape(n_tok, d)
    row = lambda v: v.reshape(1, -1).astype(F32)
    l = 0
    proj = _inproj(x2, row(norm_mix[l]), w_in[l].astype(BF16))
    attn = _attention(proj, _attn_bias_table(rel_bias[l]))
    merged = _mix(attn, proj, row(b_gate[l]), w_attn_proj[l].astype(BF16),
                  w_pool_proj[l].astype(BF16), pool_w[l].astype(BF16), row(pool_scale[l]))
    rw = router_w[l].astype(F32)
    rw_hi = rw.astype(BF16)
    rw_lo = (rw - rw_hi.astype(F32)).astype(BF16)
    x1, h, meta, cnt = _post(x2, merged, w_out[l].astype(BF16), row(norm_ffn[l]),
                             rw_hi, rw_lo, row(router_b[l]))
    dest, block_e, nsub, nreal, next_e, fills = _routing_tables(meta, cnt, n_blocks)
    xs = _dispatch(fills, nreal, dest, h, n_rows)
    y = _experts(block_e, nsub, nreal, next_e, xs, w1[l],
                 b1[l][:, None, 0::2].astype(F32), b1[l][:, None, 1::2].astype(F32),
                 w2[l], b2[l][:, None, :].astype(F32))
    out = _combine(dest, x1, meta, row(norm_final), y)
    return out.reshape(bsz, seq, d)
```

```python
import functools

import jax
import jax.numpy as jnp
from jax import lax
from jax.experimental import pallas as pl
from jax.experimental.pallas import tpu as pltpu

D_MODEL = 2048
CHUNK = 64
LEFT_CHUNKS = 8
BAND = (LEFT_CHUNKS + 1) * CHUNK
ATTN_WIDTH = D_MODEL // 2
HEAD_DIM = 64
ATTN_HEADS = ATTN_WIDTH // HEAD_DIM
MAX_REL = 256
POOL_WINDOWS = (2, 4, 8, 16)
POOL_GROUPS = len(POOL_WINDOWS)
POOL_WIDTH = D_MODEL // 2
POOL_GROUP_DIM = POOL_WIDTH // POOL_GROUPS
N_BRANCHES = 2
IN_WIDTH = 3 * ATTN_WIDTH + POOL_WIDTH + N_BRANCHES * D_MODEL
N_EXPERTS = 32
TOP_K = 4
D_FF = D_MODEL
SWIGLU_LIMIT = 7.0
SWIGLU_ALPHA = 1.702
EXPERT_BLOCK = 512
SUB_BLOCK = 256
SUBS_PER_BLOCK = EXPERT_BLOCK // SUB_BLOCK
assert SUBS_PER_BLOCK == 2
UP_HALF = D_FF
SPLIT_TILE = 256
UP_COLS = 512
RMS_EPS = 1e-5
NEG_INF = -1e30
LOG2_E = 1.4426950408889634

LANES = 128
MAX_HALO = max(POOL_WINDOWS)

TM_IN, TN_IN = 1024, 2048
Q_GROUP = 4
TQ = Q_GROUP * CHUNK
KV_BLOCKS = LEFT_CHUNKS // Q_GROUP + 1
KV_SPAN = KV_BLOCKS * TQ
TM_MIX = 512
TM_POST = 512
TM_DISP = 1024
DRAIN_ROWS = 512
TM_COMB = 256

MIB = 1024 * 1024
F32 = jnp.float32
BF16 = jnp.bfloat16


def _cparams(sem, vmem_mib):
    return pltpu.CompilerParams(dimension_semantics=sem, vmem_limit_bytes=vmem_mib * MIB)


def _resident(shape, index_map):
    return pl.BlockSpec(shape, index_map, pipeline_mode=pl.Buffered(1))


def _inproj_kernel(x_ref, g_ref, w_ref, o_ref, xn_ref):
    @pl.when(pl.program_id(1) == 0)
    def _():
        x = x_ref[...]
        var = jnp.mean(x * x, axis=-1, keepdims=True)
        xn_ref[...] = (x * lax.rsqrt(var + RMS_EPS) * g_ref[...]).astype(BF16)

    o_ref[...] = jnp.dot(xn_ref[...], w_ref[...], preferred_element_type=F32).astype(o_ref.dtype)


def _inproj(x2, g, w_bf):
    s = x2.shape[0]
    return pl.pallas_call(
        _inproj_kernel,
        grid=(s // TM_IN, IN_WIDTH // TN_IN),
        in_specs=[
            pl.BlockSpec((TM_IN, D_MODEL), lambda i, j: (i, 0)),
            pl.BlockSpec((1, D_MODEL), lambda i, j: (0, 0)),
            pl.BlockSpec((D_MODEL, TN_IN), lambda i, j: (0, j)),
        ],
        out_specs=pl.BlockSpec((TM_IN, TN_IN), lambda i, j: (i, j)),
        out_shape=jax.ShapeDtypeStruct((s, IN_WIDTH), BF16),
        scratch_shapes=[pltpu.VMEM((TM_IN, D_MODEL), BF16)],
        compiler_params=_cparams(("parallel", "arbitrary"), 56),
        name="inproj",
    )(x2, g, w_bf)


def _attn_heads(q_ref, k_refs, v_refs, bias_ref, o_ref, valid):
    lane = lax.broadcasted_iota(jnp.int32, (1, LANES), 1)
    scale = HEAD_DIM ** -0.5 * LOG2_E
    nt = (((1,), (1,)), ((), ()))
    for hp in range(ATTN_HEADS // 2):
        cs = slice(hp * LANES, (hp + 1) * LANES)
        qp = q_ref[:, cs]
        ks = [r[:, cs] for r in k_refs]
        vs = [r[:, cs] for r in v_refs]
        zero = jnp.zeros_like(qp)
        q2 = jnp.concatenate([jnp.where(lane < HEAD_DIM, qp, zero),
                              jnp.where(lane >= HEAD_DIM, qp, zero)], axis=0)
        s = jnp.concatenate(
            [lax.dot_general(q2, kb, nt, preferred_element_type=F32) for kb in ks], axis=1)
        s = s * scale + bias_ref[hp]
        if valid is not None:
            s = jnp.where(valid, s, NEG_INF)
        m = jnp.max(s, axis=-1, keepdims=True)
        p = jnp.exp2(s - m)
        l = jnp.sum(p, axis=-1, keepdims=True)
        pb = p.astype(BF16)
        o = jnp.dot(pb[:, 0:TQ], vs[0], preferred_element_type=F32)
        o += jnp.dot(pb[:, TQ:2 * TQ], vs[1], preferred_element_type=F32)
        o += jnp.dot(pb[:, 2 * TQ:3 * TQ], vs[2], preferred_element_type=F32)
        o = o / l
        o_ref[:, cs] = jnp.where(lane < HEAD_DIM, o[:TQ], o[TQ:]).astype(o_ref.dtype)


def _attn_kernel(q_ref, k0_ref, k1_ref, k2_ref, v0_ref, v1_ref, v2_ref, bias_ref, o_ref):
    i = pl.program_id(0)
    k_refs = (k0_ref, k1_ref, k2_ref)
    v_refs = (v0_ref, v1_ref, v2_ref)
    first_full = LEFT_CHUNKS * CHUNK // TQ

    @pl.when(i < first_full)
    def _():
        col = lax.broadcasted_iota(jnp.int32, (1, KV_SPAN), 1)
        _attn_heads(q_ref, k_refs, v_refs, bias_ref, o_ref, col >= (LEFT_CHUNKS * CHUNK - TQ * i))

    @pl.when(i >= first_full)
    def _():
        _attn_heads(q_ref, k_refs, v_refs, bias_ref, o_ref, None)


def _attn_bias_table(rel_bias):
    span = BAND + CHUNK - 1
    dist = LEFT_CHUNKS * CHUNK + (CHUNK - 1) - jnp.arange(span)
    by_col = rel_bias.astype(F32)[:, jnp.clip(dist, -(CHUNK - 1), MAX_REL) + (CHUNK - 1)]
    skew = jnp.tile(by_col, (1, CHUNK + 1))[:, :CHUNK * (span + 1)].reshape(-1, CHUNK, span + 1)
    band_bias = skew[:, ::-1, :BAND]
    rows = [
        jnp.pad(band_bias, ((0, 0), (0, 0), (c * CHUNK, KV_SPAN - BAND - c * CHUNK)),
                constant_values=NEG_INF)
        for c in range(Q_GROUP)
    ]
    table = jnp.concatenate(rows, axis=1) * LOG2_E
    return table.reshape(ATTN_HEADS // 2, 2 * TQ, KV_SPAN)


def _attention(proj, bias_tab):
    s = proj.shape[0]
    kcol, vcol = 1, 2

    def kv_spec(back, colblk):
        return pl.BlockSpec((TQ, ATTN_WIDTH), lambda i: (jnp.maximum(i - back, 0), colblk))

    return pl.pallas_call(
        _attn_kernel,
        grid=(s // TQ,),
        in_specs=[
            pl.BlockSpec((TQ, ATTN_WIDTH), lambda i: (i, 0)),
            kv_spec(2, kcol), kv_spec(1, kcol), kv_spec(0, kcol),
            kv_spec(2, vcol), kv_spec(1, vcol), kv_spec(0, vcol),
            _resident((ATTN_HEADS // 2, 2 * TQ, KV_SPAN), lambda i: (0, 0, 0)),
        ],
        out_specs=pl.BlockSpec((TQ, ATTN_WIDTH), lambda i: (i, 0)),
        out_shape=jax.ShapeDtypeStruct((s, ATTN_WIDTH), BF16),
        compiler_params=_cparams(("parallel",), 48),
        name="attn",
    )(proj, proj, proj, proj, proj, proj, proj, bias_tab)


def _mix_kernel(attn_ref, pin_ref, gl_ref, bg_ref, wap_ref, wpp_ref, pw_ref, ps_ref,
                o_ref, halo_ref, win_ref, hwin_ref):
    i = pl.program_id(0)
    tm = TM_MIX

    @pl.when(i == 0)
    def _():
        halo_ref[...] = jnp.zeros_like(halo_ref)
        t = lax.broadcasted_iota(jnp.int32, (tm, tm), 0)
        j = lax.broadcasted_iota(jnp.int32, (tm, tm), 1)
        th = lax.broadcasted_iota(jnp.int32, (tm, MAX_HALO), 0)
        jh = lax.broadcasted_iota(jnp.int32, (tm, MAX_HALO), 1)
        for gi, w in enumerate(POOL_WINDOWS):
            win_ref[gi] = jnp.where((t - j >= 0) & (t - j < w), 1.0, 0.0).astype(BF16)
            hwin_ref[gi] = jnp.where(th + MAX_HALO - jh < w, 1.0, 0.0).astype(BF16)

    pos = i * tm + lax.broadcasted_iota(jnp.int32, (tm, 1), 0)
    u_all = pin_ref[...]
    halo = halo_ref[...]
    mixed = []
    for gi, w in enumerate(POOL_WINDOWS):
        cs = slice(gi * POOL_GROUP_DIM, (gi + 1) * POOL_GROUP_DIM)
        u = u_all[:, cs]
        wsum = jnp.dot(win_ref[gi], u, preferred_element_type=F32)
        wsum += jnp.dot(hwin_ref[gi], halo[:, cs], preferred_element_type=F32)
        inv_cnt = jnp.where(pos + 1 >= w, 1.0 / w, 1.0 / jnp.minimum(pos + 1, w).astype(F32))
        pooled = wsum * inv_cnt - u.astype(F32)
        mixed.append(jnp.dot(pooled.astype(BF16), pw_ref[gi], preferred_element_type=F32))
    mixed = jnp.concatenate(mixed, axis=1) * ps_ref[...]
    halo_ref[...] = u_all[tm - MAX_HALO:, :]

    y_pool = jnp.dot(mixed.astype(BF16), wpp_ref[...], preferred_element_type=F32)
    y_attn = jnp.dot(attn_ref[...], wap_ref[...], preferred_element_type=F32)
    gates = jax.nn.sigmoid(gl_ref[...].astype(F32) + bg_ref[...])
    merged = gates[:, :D_MODEL] * y_attn + gates[:, D_MODEL:] * y_pool
    o_ref[...] = merged.astype(o_ref.dtype)


def _mix(attn, proj, b_gate, wap_bf, wpp_bf, pw_bf, pool_scale):
    s = attn.shape[0]
    tm = TM_MIX
    pin_col = 3 * ATTN_WIDTH // POOL_WIDTH
    gl_col = (3 * ATTN_WIDTH + POOL_WIDTH) // (N_BRANCHES * D_MODEL)
    return pl.pallas_call(
        _mix_kernel,
        grid=(s // tm,),
        in_specs=[
            pl.BlockSpec((tm, ATTN_WIDTH), lambda i: (i, 0)),
            pl.BlockSpec((tm, POOL_WIDTH), lambda i: (i, pin_col)),
            pl.BlockSpec((tm, N_BRANCHES * D_MODEL), lambda i: (i, gl_col)),
            pl.BlockSpec((1, N_BRANCHES * D_MODEL), lambda i: (0, 0)),
            _resident((ATTN_WIDTH, D_MODEL), lambda i: (0, 0)),
            _resident((POOL_WIDTH, D_MODEL), lambda i: (0, 0)),
            _resident((POOL_GROUPS, POOL_GROUP_DIM, POOL_GROUP_DIM), lambda i: (0, 0, 0)),
            pl.BlockSpec((1, POOL_WIDTH), lambda i: (0, 0)),
        ],
        out_specs=pl.BlockSpec((tm, D_MODEL), lambda i: (i, 0)),
        out_shape=jax.ShapeDtypeStruct((s, D_MODEL), BF16),
        scratch_shapes=[
            pltpu.VMEM((MAX_HALO, POOL_WIDTH), BF16),
            pltpu.VMEM((POOL_GROUPS, tm, tm), BF16),
            pltpu.VMEM((POOL_GROUPS, tm, MAX_HALO), BF16),
        ],
        compiler_params=_cparams(("arbitrary",), 48),
        name="mix",
    )(attn, proj, proj, b_gate, wap_bf, wpp_bf, pw_bf, pool_scale)


META_E, META_POS, META_W = 0, TOP_K, 2 * TOP_K


def _post_kernel(x_ref, m_ref, wout_ref, g_ref, rwh_ref, rwl_ref, rb_ref,
                 x1_ref, h_ref, meta_ref, cnt_ref, carry_ref, tri_ref):
    i = pl.program_id(0)
    tm = TM_POST

    @pl.when(i == 0)
    def _():
        carry_ref[...] = jnp.zeros_like(carry_ref)
        t = lax.broadcasted_iota(jnp.int32, (tm, tm), 0)
        j = lax.broadcasted_iota(jnp.int32, (tm, tm), 1)
        tri_ref[...] = jnp.where(j < t, 1.0, 0.0).astype(BF16)

    x1 = x_ref[...] + jnp.dot(m_ref[...], wout_ref[...], preferred_element_type=F32)
    x1_ref[...] = x1
    var = jnp.mean(x1 * x1, axis=-1, keepdims=True)
    h = x1 * lax.rsqrt(var + RMS_EPS) * g_ref[...]
    h_ref[...] = h

    h_hi = h.astype(BF16)
    h_lo = (h - h_hi.astype(F32)).astype(BF16)
    logits = (jnp.dot(h_hi, rwh_ref[...], preferred_element_type=F32)
              + jnp.dot(h_hi, rwl_ref[...], preferred_element_type=F32)
              + jnp.dot(h_lo, rwh_ref[...], preferred_element_type=F32)
              + rb_ref[...])

    lane_e = lax.broadcasted_iota(jnp.int32, (tm, N_EXPERTS), 1).astype(F32)
    work = logits
    sel = jnp.zeros((tm, N_EXPERTS), F32)
    picks = []
    for _ in range(TOP_K):
        m = jnp.max(work, axis=-1, keepdims=True)
        idx = jnp.min(jnp.where(work == m, lane_e, float(N_EXPERTS)), axis=-1, keepdims=True)
        onehot = lane_e == idx
        picks.append((m, idx, onehot))
        sel = jnp.where(onehot, 1.0, sel)
        work = jnp.where(onehot, -jnp.inf, work)

    rank = jnp.dot(tri_ref[...], sel.astype(BF16), preferred_element_type=F32) + carry_ref[...]
    carry_ref[...] += jnp.sum(sel, axis=0, keepdims=True)
    cnt_ref[...] = carry_ref[...]

    top = picks[0][0]
    exps = [jnp.exp(m - top) for m, _, _ in picks]
    denom = exps[0] + exps[1] + exps[2] + exps[3]
    lane = lax.broadcasted_iota(jnp.int32, (tm, LANES), 1)
    meta = jnp.zeros((tm, LANES), F32)
    for k, (m, idx, onehot) in enumerate(picks):
        pos_k = jnp.sum(jnp.where(onehot, rank, 0.0), axis=-1, keepdims=True)
        meta = jnp.where(lane == META_E + k, idx, meta)
        meta = jnp.where(lane == META_POS + k, pos_k, meta)
        meta = jnp.where(lane == META_W + k, exps[k] / denom, meta)
    meta_ref[...] = meta


def _post(x2, merged, wout_bf, g, rw_hi, rw_lo, rb):
    s = x2.shape[0]
    tm = TM_POST
    return pl.pallas_call(
        _post_kernel,
        grid=(s // tm,),
        in_specs=[
            pl.BlockSpec((tm, D_MODEL), lambda i: (i, 0)),
            pl.BlockSpec((tm, D_MODEL), lambda i: (i, 0)),
            _resident((D_MODEL, D_MODEL), lambda i: (0, 0)),
            pl.BlockSpec((1, D_MODEL), lambda i: (0, 0)),
            pl.BlockSpec((D_MODEL, N_EXPERTS), lambda i: (0, 0)),
            pl.BlockSpec((D_MODEL, N_EXPERTS), lambda i: (0, 0)),
            pl.BlockSpec((1, N_EXPERTS), lambda i: (0, 0)),
        ],
        out_specs=[
            pl.BlockSpec((tm, D_MODEL), lambda i: (i, 0)),
            pl.BlockSpec((tm, D_MODEL), lambda i: (i, 0)),
            pl.BlockSpec((tm, LANES), lambda i: (i, 0)),
            pl.BlockSpec((1, N_EXPERTS), lambda i: (0, 0)),
        ],
        out_shape=[
            jax.ShapeDtypeStruct((s, D_MODEL), F32),
            jax.ShapeDtypeStruct((s, D_MODEL), F32),
            jax.ShapeDtypeStruct((s, LANES), F32),
            jax.ShapeDtypeStruct((1, N_EXPERTS), F32),
        ],
        scratch_shapes=[
            pltpu.VMEM((1, N_EXPERTS), F32),
            pltpu.VMEM((tm, tm), BF16),
        ],
        compiler_params=_cparams(("arbitrary",), 48),
        name="post",
    )(x2, merged, wout_bf, g, rw_hi, rw_lo, rb)


def _row_copy(src, src_row, dst, dst_row, sem):
    return pltpu.make_async_copy(src.at[pl.ds(src_row, 1)], dst.at[pl.ds(dst_row, 1)], sem)


def _rows_copy(src, dst, dst_row, n_rows, sem):
    return pltpu.make_async_copy(src.at[pl.ds(0, n_rows)],
                                 dst.at[pl.ds(pl.multiple_of(dst_row, SUB_BLOCK), n_rows)], sem)


def _dispatch_kernel(padbase_ref, padcnt_ref, subbase_ref, subcnt_ref, nreal_ref,
                     dest_ref, h_ref, zero_hbm, xs_hbm, sem, pad_sem):
    i = pl.program_id(0)
    tm = TM_DISP
    n_blocks = xs_hbm.shape[0] // EXPERT_BLOCK

    for t in range(tm):
        for k in range(TOP_K):
            _row_copy(h_ref, t, xs_hbm, dest_ref[0, 0, TOP_K * t + k], sem).start(priority=k % 2)

    @pl.when(i == 0)
    def _():
        for e in range(N_EXPERTS):
            def fill(r, c, e=e):
                _row_copy(zero_hbm, 0, xs_hbm, padbase_ref[e] + r, pad_sem).start()
                return c
            lax.fori_loop(0, padcnt_ref[e], fill, 0)

            def fill_sub(r, c, e=e):
                _rows_copy(zero_hbm, xs_hbm, subbase_ref[e], SUB_BLOCK, pad_sem).start()
                return c
            lax.fori_loop(0, subcnt_ref[e], fill_sub, 0)

        def fill_block(b, c):
            _rows_copy(zero_hbm, xs_hbm, b * EXPERT_BLOCK, EXPERT_BLOCK, pad_sem).start()
            return c
        lax.fori_loop(nreal_ref[0], n_blocks, fill_block, 0)

        for e in range(N_EXPERTS):
            def drain(r, c):
                _row_copy(zero_hbm, 0, xs_hbm, 0, pad_sem).wait()
                return c
            lax.fori_loop(0, padcnt_ref[e], drain, 0)

            def drain_sub(r, c):
                _rows_copy(zero_hbm, xs_hbm, 0, SUB_BLOCK, pad_sem).wait()
                return c
            lax.fori_loop(0, subcnt_ref[e], drain_sub, 0)

        def drain_block(b, c):
            _rows_copy(zero_hbm, xs_hbm, 0, EXPERT_BLOCK, pad_sem).wait()
            return c
        lax.fori_loop(nreal_ref[0], n_blocks, drain_block, 0)

    for _ in range(TOP_K * tm // DRAIN_ROWS):
        pltpu.make_async_copy(h_ref.at[pl.ds(0, DRAIN_ROWS)], xs_hbm.at[pl.ds(0, DRAIN_ROWS)],
                              sem).wait()


def _dispatch(fills, nreal, dest, h, n_rows):
    s = h.shape[0]
    tm = TM_DISP
    dest3 = dest.reshape(s // tm, 1, TOP_K * tm)
    zero_rows = jnp.zeros((EXPERT_BLOCK, D_MODEL), F32)
    grid_spec = pltpu.PrefetchScalarGridSpec(
        num_scalar_prefetch=5,
        grid=(s // tm,),
        in_specs=[
            pl.BlockSpec((1, 1, TOP_K * tm), lambda i, *_: (i, 0, 0), memory_space=pltpu.SMEM),
            pl.BlockSpec((tm, D_MODEL), lambda i, *_: (i, 0)),
            pl.BlockSpec(memory_space=pl.ANY),
        ],
        out_specs=pl.BlockSpec(memory_space=pl.ANY),
        scratch_shapes=[pltpu.SemaphoreType.DMA, pltpu.SemaphoreType.DMA],
    )
    return pl.pallas_call(
        _dispatch_kernel,
        grid_spec=grid_spec,
        out_shape=jax.ShapeDtypeStruct((n_rows, D_MODEL), F32),
        compiler_params=_cparams(("arbitrary",), 32),
        name="dispatch",
    )(*fills, nreal, dest3, h, zero_rows)


def _per_sub_block(nsub, o_ref, compute):
    @pl.when(nsub == SUBS_PER_BLOCK)
    def _():
        compute(slice(0, EXPERT_BLOCK))

    for n in range(SUBS_PER_BLOCK):
        @pl.when(nsub == n)
        def _(n=n):
            if n:
                compute(slice(0, n * SUB_BLOCK))
            o_ref[n * SUB_BLOCK:, :] = jnp.zeros((EXPERT_BLOCK - n * SUB_BLOCK, o_ref.shape[1]),
                                                 o_ref.dtype)


def _up_kernel(be_ref, nsub_ref, nreal_ref, nexte_ref, x_ref, w_hbm, bg_ref, bu_ref, perm_ref,
               a_ref, stage_ref, wg_ref, wu_ref, sem):
    half = pl.program_id(0)
    b = pl.program_id(1)
    e = be_ref[b]

    def fetch(expert, hf):
        cols = pl.ds(pl.multiple_of(hf * UP_HALF, UP_HALF), UP_HALF)
        return pltpu.make_async_copy(w_hbm.at[expert, :, cols], stage_ref, sem)

    @pl.when((half == 0) & (b == 0))
    def _():
        fetch(e, 0).start()

    @pl.when((b == 0) | (e != be_ref[jnp.maximum(b - 1, 0)]))
    def _():
        fetch(e, half).wait()
        for t in range(UP_HALF // SPLIT_TILE):
            w = stage_ref[:, t * SPLIT_TILE:(t + 1) * SPLIT_TILE].astype(BF16)
            r = jnp.dot(w, perm_ref[...], preferred_element_type=F32)
            cs = slice(t * (SPLIT_TILE // 2), (t + 1) * (SPLIT_TILE // 2))
            wg_ref[:, cs] = r[:, :SPLIT_TILE // 2].astype(BF16)
            wu_ref[:, cs] = r[:, SPLIT_TILE // 2:].astype(BF16)
        last_run = nexte_ref[b] == e

        @pl.when(jnp.logical_not(last_run))
        def _():
            fetch(nexte_ref[b], half).start()

        @pl.when(last_run & (half == 0))
        def _():
            fetch(be_ref[0], 1).start()

    def swiglu(rows):
        x = x_ref[rows, :].astype(BF16)
        for c in range(UP_HALF // 2 // UP_COLS):
            cs = slice(c * UP_COLS, (c + 1) * UP_COLS)
            g = jnp.dot(x, wg_ref[:, cs], preferred_element_type=F32) + bg_ref[0, :, cs]
            u = jnp.dot(x, wu_ref[:, cs], preferred_element_type=F32) + bu_ref[0, :, cs]
            g = jnp.minimum(g, SWIGLU_LIMIT)
            u = jnp.clip(u, -SWIGLU_LIMIT, SWIGLU_LIMIT)
            a_ref[rows, cs] = (g * jax.nn.sigmoid(SWIGLU_ALPHA * g) * (u + 1.0)).astype(a_ref.dtype)

    _per_sub_block(nsub_ref[b], a_ref, swiglu)


def _down_kernel(be_ref, nsub_ref, nreal_ref, nexte_ref, a_ref, w_hbm, b_ref, y_ref,
                 stage_ref, wbf_ref, sem):
    b = pl.program_id(0)
    e = be_ref[b]

    def fetch(expert):
        return pltpu.make_async_copy(w_hbm.at[expert], stage_ref, sem)

    @pl.when(b == 0)
    def _():
        fetch(e).start()

    @pl.when((b == 0) | (e != be_ref[jnp.maximum(b - 1, 0)]))
    def _():
        fetch(e).wait()
        wbf_ref[...] = stage_ref[...].astype(BF16)

        @pl.when(nexte_ref[b] != e)
        def _():
            fetch(nexte_ref[b]).start()

    def project(rows):
        y_ref[rows, :] = jnp.dot(a_ref[rows, :], wbf_ref[...], preferred_element_type=F32) + b_ref[0]

    _per_sub_block(nsub_ref[b], y_ref, project)


def _row_block(b, be, ns, nr, *_):
    return (jnp.minimum(b, nr[0] - 1), 0)


def _expert_block(b, be, *_):
    return (be[b], 0, 0)


def _experts(block_e, nsub, nreal, next_e, xs, w1, b1g, b1u, w2, b2):
    n_rows = xs.shape[0]
    nb = n_rows // EXPERT_BLOCK
    i = jnp.arange(SPLIT_TILE)[:, None]
    c = jnp.arange(SPLIT_TILE)[None, :]
    perm = jnp.where(c < SPLIT_TILE // 2, i == 2 * c, i == 2 * (c - SPLIT_TILE // 2) + 1).astype(BF16)
    act = pl.pallas_call(
        _up_kernel,
        grid_spec=pltpu.PrefetchScalarGridSpec(
            num_scalar_prefetch=4,
            grid=(2 * D_FF // UP_HALF, nb),
            in_specs=[
                pl.BlockSpec((EXPERT_BLOCK, D_MODEL),
                             lambda hf, b, be, ns, nr, ne: (jnp.minimum(b, nr[0] - 1), 0)),
                pl.BlockSpec(memory_space=pl.ANY),
                pl.BlockSpec((1, 1, UP_HALF // 2), lambda hf, b, be, *_: (be[b], 0, hf)),
                pl.BlockSpec((1, 1, UP_HALF // 2), lambda hf, b, be, *_: (be[b], 0, hf)),
                pl.BlockSpec((SPLIT_TILE, SPLIT_TILE), lambda hf, b, *_: (0, 0)),
            ],
            out_specs=pl.BlockSpec((EXPERT_BLOCK, UP_HALF // 2), lambda hf, b, *_: (b, hf)),
            scratch_shapes=[
                pltpu.VMEM((D_MODEL, UP_HALF), F32),
                pltpu.VMEM((D_MODEL, UP_HALF // 2), BF16),
                pltpu.VMEM((D_MODEL, UP_HALF // 2), BF16),
                pltpu.SemaphoreType.DMA,
            ],
        ),
        out_shape=jax.ShapeDtypeStruct((n_rows, D_FF), BF16),
        compiler_params=_cparams(("arbitrary", "arbitrary"), 56),
        name="expert_up",
    )(block_e, nsub, nreal, next_e, xs, w1, b1g, b1u, perm)
    return pl.pallas_call(
        _down_kernel,
        grid_spec=pltpu.PrefetchScalarGridSpec(
            num_scalar_prefetch=4,
            grid=(nb,),
            in_specs=[
                pl.BlockSpec((EXPERT_BLOCK, D_FF), _row_block),
                pl.BlockSpec(memory_space=pl.ANY),
                pl.BlockSpec((1, 1, D_MODEL), _expert_block),
            ],
            out_specs=pl.BlockSpec((EXPERT_BLOCK, D_MODEL), lambda b, *_: (b, 0)),
            scratch_shapes=[
                pltpu.VMEM((D_FF, D_MODEL), F32),
                pltpu.VMEM((D_FF, D_MODEL), BF16),
                pltpu.SemaphoreType.DMA,
            ],
        ),
        out_shape=jax.ShapeDtypeStruct((n_rows, D_MODEL), F32),
        compiler_params=_cparams(("arbitrary",), 56),
        name="expert_down",
    )(block_e, nsub, nreal, next_e, act, w2, b2)


def _combine_kernel(dcur_ref, dnxt_ref, x1_ref, meta_ref, g_ref, y_hbm, o_ref, ybuf, sem):
    i = pl.program_id(0)
    n = pl.num_programs(0)
    tm = TM_COMB
    slot = i % 2

    def row_gather(dref, s, t, k):
        return pltpu.make_async_copy(y_hbm.at[pl.ds(dref[0, 0, TOP_K * t + k], 1)],
                                     ybuf.at[s, k, pl.ds(t, 1)], sem.at[s])

    @pl.when(i == 0)
    def _():
        def body(t, c):
            for k in range(TOP_K):
                row_gather(dcur_ref, 0, t, k).start()
            return c
        lax.fori_loop(0, tm, body, 0, unroll=8)

    for s in range(2):
        @pl.when((i + 1 < n) & (slot == 1 - s))
        def _(s=s):
            for t in range(tm):
                for k in range(TOP_K):
                    row_gather(dnxt_ref, s, t, k).start(priority=k % 2)

    for k in range(TOP_K):
        pltpu.make_async_copy(y_hbm.at[pl.ds(0, tm)], ybuf.at[slot, k], sem.at[slot]).wait()

    meta = meta_ref[...]
    acc = x1_ref[...]
    for k in range(TOP_K):
        acc = acc + meta[:, META_W + k:META_W + k + 1] * ybuf[slot, k]
    var = jnp.mean(acc * acc, axis=-1, keepdims=True)
    o_ref[...] = (acc * lax.rsqrt(var + RMS_EPS) * g_ref[...]).astype(o_ref.dtype)


def _combine(dest, x1, meta, g, y):
    s = x1.shape[0]
    tm = TM_COMB
    nt = s // tm
    dest3 = dest.reshape(nt, 1, TOP_K * tm)
    return pl.pallas_call(
        _combine_kernel,
        grid=(nt,),
        in_specs=[
            pl.BlockSpec((1, 1, TOP_K * tm), lambda i: (i, 0, 0), memory_space=pltpu.SMEM),
            pl.BlockSpec((1, 1, TOP_K * tm), lambda i: (jnp.minimum(i + 1, nt - 1), 0, 0),
                         memory_space=pltpu.SMEM),
            pl.BlockSpec((tm, D_MODEL), lambda i: (i, 0)),
            pl.BlockSpec((tm, LANES), lambda i: (i, 0)),
            pl.BlockSpec((1, D_MODEL), lambda i: (0, 0)),
            pl.BlockSpec(memory_space=pl.ANY),
        ],
        out_specs=pl.BlockSpec((tm, D_MODEL), lambda i: (i, 0)),
        out_shape=jax.ShapeDtypeStruct((s, D_MODEL), F32),
        scratch_shapes=[
            pltpu.VMEM((2, TOP_K, tm, D_MODEL), F32),
            pltpu.SemaphoreType.DMA((2,)),
        ],
        compiler_params=_cparams(("arbitrary",), 40),
        name="combine",
    )(dest3, dest3, x1, meta, g, y)


def _routing_tables(meta, cnt, n_blocks):
    counts = cnt[0].astype(jnp.int32)
    padded = ((counts + EXPERT_BLOCK - 1) // EXPERT_BLOCK) * EXPERT_BLOCK
    pad_end = jnp.cumsum(padded)
    pad_start = pad_end - padded
    e4 = meta[:, META_E:META_E + TOP_K].astype(jnp.int32)
    pos4 = meta[:, META_POS:META_POS + TOP_K].astype(jnp.int32)
    experts = jnp.arange(N_EXPERTS, dtype=jnp.int32)
    dest = pos4 + jnp.sum(jnp.where(e4[..., None] == experts, pad_start, 0), axis=-1)
    nreal = pad_end[-1:] // EXPERT_BLOCK
    blk_raw = jnp.arange(n_blocks, dtype=jnp.int32)
    blk = jnp.minimum(blk_raw, nreal[0] - 1)
    block_e = jnp.sum(pad_end[None, :] <= (blk * EXPERT_BLOCK)[:, None], axis=1).astype(jnp.int32)
    onehot_e = block_e[:, None] == experts
    pick = lambda table: jnp.sum(jnp.where(onehot_e, table, 0), axis=1)
    rows_left = pick(counts) - (blk * EXPERT_BLOCK - pick(pad_start))
    nsub = jnp.clip((rows_left + SUB_BLOCK - 1) // SUB_BLOCK, 0, SUBS_PER_BLOCK)
    nsub = jnp.where(blk_raw < nreal[0], nsub, 0)
    later = experts[None, :] > block_e[:, None]
    next_e = jnp.min(jnp.where(later & (counts > 0)[None, :], experts, N_EXPERTS), axis=1)
    next_e = jnp.where(next_e == N_EXPERTS, block_e, next_e)
    sub_rows = ((counts + SUB_BLOCK - 1) // SUB_BLOCK) * SUB_BLOCK
    fills = (pad_start + counts, sub_rows - counts, pad_start + sub_rows,
             (padded - sub_rows) // SUB_BLOCK)
    i32 = lambda v: v.astype(jnp.int32)
    return dest, i32(block_e), i32(nsub), i32(nreal), i32(next_e), tuple(i32(f) for f in fills)


def kernel(x, norm_mix, w_in, b_gate, rel_bias, w_attn_proj, pool_w, pool_scale, w_pool_proj,
           w_out, norm_ffn, router_w, router_b, w1, b1, w2, b2, norm_final):
    bsz, seq, d = x.shape
    n_tok = bsz * seq
    assert w_in.shape[0] == 1 and bsz == 1 and d == D_MODEL and seq % TM_IN == 0
    n_blocks = -(-(n_tok * TOP_K) // EXPERT_BLOCK) + N_EXPERTS
    n_rows = n_blocks * EXPERT_BLOCK
    x2 = x.reshape(n_tok, d)
    row = lambda v: v.reshape(1, -1).astype(F32)
    l = 0
    proj = _inproj(x2, row(norm_mix[l]), w_in[l].astype(BF16))
    attn = _attention(proj, _attn_bias_table(rel_bias[l]))
    merged = _mix(attn, proj, row(b_gate[l]), w_attn_proj[l].astype(BF16),
                  w_pool_proj[l].astype(BF16), pool_w[l].astype(BF16), row(pool_scale[l]))
    rw = router_w[l].astype(F32)
    rw_hi = rw.astype(BF16)
    rw_lo = (rw - rw_hi.astype(F32)).astype(BF16)
    x1, h, meta, cnt = _post(x2, merged, w_out[l].astype(BF16), row(norm_ffn[l]),
                             rw_hi, rw_lo, row(router_b[l]))
    dest, block_e, nsub, nreal, next_e, fills = _routing_tables(meta, cnt, n_blocks)
    xs = _dispatch(fills, nreal, dest, h, n_rows)
    y = _experts(block_e, nsub, nreal, next_e, xs, w1[l],
                 b1[l][:, None, 0::2].astype(F32), b1[l][:, None, 1::2].astype(F32),
                 w2[l], b2[l][:, None, :].astype(F32))
    out = _combine(dest, x1, meta, row(norm_final), y)
    return out.reshape(bsz, seq, d)
```

```python
import functools

import jax
import jax.numpy as jnp
from jax import lax
from jax.experimental import pallas as pl
from jax.experimental.pallas import tpu as pltpu

D_MODEL = 2048
CHUNK = 64
LEFT_CHUNKS = 8
BAND = (LEFT_CHUNKS + 1) * CHUNK
ATTN_WIDTH = D_MODEL // 2
HEAD_DIM = 64
ATTN_HEADS = ATTN_WIDTH // HEAD_DIM
MAX_REL = 256
POOL_WINDOWS = (2, 4, 8, 16)
POOL_GROUPS = len(POOL_WINDOWS)
POOL_WIDTH = D_MODEL // 2
POOL_GROUP_DIM = POOL_WIDTH // POOL_GROUPS
N_BRANCHES = 2
IN_WIDTH = 3 * ATTN_WIDTH + POOL_WIDTH + N_BRANCHES * D_MODEL
N_EXPERTS = 32
TOP_K = 4
D_FF = D_MODEL
SWIGLU_LIMIT = 7.0
SWIGLU_ALPHA = 1.702
EXPERT_BLOCK = 1024
SUB_BLOCK = 256
SUBS_PER_BLOCK = EXPERT_BLOCK // SUB_BLOCK
UP_HALF = D_FF
SPLIT_TILE = 256
UP_COLS = 512
RMS_EPS = 1e-5
NEG_INF = -1e30
LOG2_E = 1.4426950408889634

LANES = 128
MAX_HALO = max(POOL_WINDOWS)

TM_IN, TN_IN = 1024, 2048
Q_GROUP = 4
TQ = Q_GROUP * CHUNK
KV_BLOCKS = LEFT_CHUNKS // Q_GROUP + 1
KV_SPAN = KV_BLOCKS * TQ
TM_MIX = 512
TM_POST = 512
TM_DISP = 1024
DRAIN_ROWS = 512
TM_COMB = 256

MIB = 1024 * 1024
F32 = jnp.float32
BF16 = jnp.bfloat16


def _cparams(sem, vmem_mib):
    return pltpu.CompilerParams(dimension_semantics=sem, vmem_limit_bytes=vmem_mib * MIB)


def _resident(shape, index_map):
    return pl.BlockSpec(shape, index_map, pipeline_mode=pl.Buffered(1))


def _inproj_kernel(x_ref, g_ref, w_ref, o_ref, xn_ref):
    @pl.when(pl.program_id(1) == 0)
    def _():
        x = x_ref[...]
        var = jnp.mean(x * x, axis=-1, keepdims=True)
        xn_ref[...] = (x * lax.rsqrt(var + RMS_EPS) * g_ref[...]).astype(BF16)

    o_ref[...] = jnp.dot(xn_ref[...], w_ref[...], preferred_element_type=F32).astype(o_ref.dtype)


def _inproj(x2, g, w_bf):
    s = x2.shape[0]
    return pl.pallas_call(
        _inproj_kernel,
        grid=(s // TM_IN, IN_WIDTH // TN_IN),
        in_specs=[
            pl.BlockSpec((TM_IN, D_MODEL), lambda i, j: (i, 0)),
            pl.BlockSpec((1, D_MODEL), lambda i, j: (0, 0)),
            pl.BlockSpec((D_MODEL, TN_IN), lambda i, j: (0, j)),
        ],
        out_specs=pl.BlockSpec((TM_IN, TN_IN), lambda i, j: (i, j)),
        out_shape=jax.ShapeDtypeStruct((s, IN_WIDTH), BF16),
        scratch_shapes=[pltpu.VMEM((TM_IN, D_MODEL), BF16)],
        compiler_params=_cparams(("parallel", "arbitrary"), 56),
        name="inproj",
    )(x2, g, w_bf)


def _attn_heads(q_ref, k_refs, v_refs, bias_ref, o_ref, valid):
    lane = lax.broadcasted_iota(jnp.int32, (1, LANES), 1)
    scale = HEAD_DIM ** -0.5 * LOG2_E
    nt = (((1,), (1,)), ((), ()))
    for hp in range(ATTN_HEADS // 2):
        cs = slice(hp * LANES, (hp + 1) * LANES)
        qp = q_ref[:, cs]
        ks = [r[:, cs] for r in k_refs]
        vs = [r[:, cs] for r in v_refs]
        zero = jnp.zeros_like(qp)
        q2 = jnp.concatenate([jnp.where(lane < HEAD_DIM, qp, zero),
                              jnp.where(lane >= HEAD_DIM, qp, zero)], axis=0)
        s = jnp.concatenate(
            [lax.dot_general(q2, kb, nt, preferred_element_type=F32) for kb in ks], axis=1)
        s = s * scale + bias_ref[hp]
        if valid is not None:
            s = jnp.where(valid, s, NEG_INF)
        m = jnp.max(s, axis=-1, keepdims=True)
        p = jnp.exp2(s - m)
        l = jnp.sum(p, axis=-1, keepdims=True)
        pb = p.astype(BF16)
        o = jnp.dot(pb[:, 0:TQ], vs[0], preferred_element_type=F32)
        o += jnp.dot(pb[:, TQ:2 * TQ], vs[1], preferred_element_type=F32)
        o += jnp.dot(pb[:, 2 * TQ:3 * TQ], vs[2], preferred_element_type=F32)
        o = o / l
        o_ref[:, cs] = jnp.where(lane < HEAD_DIM, o[:TQ], o[TQ:]).astype(o_ref.dtype)


def _attn_kernel(q_ref, k0_ref, k1_ref, k2_ref, v0_ref, v1_ref, v2_ref, bias_ref, o_ref):
    i = pl.program_id(0)
    k_refs = (k0_ref, k1_ref, k2_ref)
    v_refs = (v0_ref, v1_ref, v2_ref)
    first_full = LEFT_CHUNKS * CHUNK // TQ

    @pl.when(i < first_full)
    def _():
        col = lax.broadcasted_iota(jnp.int32, (1, KV_SPAN), 1)
        _attn_heads(q_ref, k_refs, v_refs, bias_ref, o_ref, col >= (LEFT_CHUNKS * CHUNK - TQ * i))

    @pl.when(i >= first_full)
    def _():
        _attn_heads(q_ref, k_refs, v_refs, bias_ref, o_ref, None)


def _attn_bias_table(rel_bias):
    span = BAND + CHUNK - 1
    dist = LEFT_CHUNKS * CHUNK + (CHUNK - 1) - jnp.arange(span)
    by_col = rel_bias.astype(F32)[:, jnp.clip(dist, -(CHUNK - 1), MAX_REL) + (CHUNK - 1)]
    skew = jnp.tile(by_col, (1, CHUNK + 1))[:, :CHUNK * (span + 1)].reshape(-1, CHUNK, span + 1)
    band_bias = skew[:, ::-1, :BAND]
    rows = [
        jnp.pad(band_bias, ((0, 0), (0, 0), (c * CHUNK, KV_SPAN - BAND - c * CHUNK)),
                constant_values=NEG_INF)
        for c in range(Q_GROUP)
    ]
    table = jnp.concatenate(rows, axis=1) * LOG2_E
    return table.reshape(ATTN_HEADS // 2, 2 * TQ, KV_SPAN)


def _attention(proj, bias_tab):
    s = proj.shape[0]
    kcol, vcol = 1, 2

    def kv_spec(back, colblk):
        return pl.BlockSpec((TQ, ATTN_WIDTH), lambda i: (jnp.maximum(i - back, 0), colblk))

    return pl.pallas_call(
        _attn_kernel,
        grid=(s // TQ,),
        in_specs=[
            pl.BlockSpec((TQ, ATTN_WIDTH), lambda i: (i, 0)),
            kv_spec(2, kcol), kv_spec(1, kcol), kv_spec(0, kcol),
            kv_spec(2, vcol), kv_spec(1, vcol), kv_spec(0, vcol),
            _resident((ATTN_HEADS // 2, 2 * TQ, KV_SPAN), lambda i: (0, 0, 0)),
        ],
        out_specs=pl.BlockSpec((TQ, ATTN_WIDTH), lambda i: (i, 0)),
        out_shape=jax.ShapeDtypeStruct((s, ATTN_WIDTH), BF16),
        compiler_params=_cparams(("parallel",), 48),
        name="attn",
    )(proj, proj, proj, proj, proj, proj, proj, bias_tab)


def _mix_kernel(attn_ref, pin_ref, gl_ref, bg_ref, wap_ref, wpp_ref, pw_ref, ps_ref,
                o_ref, halo_ref, win_ref, hwin_ref):
    i = pl.program_id(0)
    tm = TM_MIX

    @pl.when(i == 0)
    def _():
        halo_ref[...] = jnp.zeros_like(halo_ref)
        t = lax.broadcasted_iota(jnp.int32, (tm, tm), 0)
        j = lax.broadcasted_iota(jnp.int32, (tm, tm), 1)
        th = lax.broadcasted_iota(jnp.int32, (tm, MAX_HALO), 0)
        jh = lax.broadcasted_iota(jnp.int32, (tm, MAX_HALO), 1)
        for gi, w in enumerate(POOL_WINDOWS):
            win_ref[gi] = jnp.where((t - j >= 0) & (t - j < w), 1.0, 0.0).astype(BF16)
            hwin_ref[gi] = jnp.where(th + MAX_HALO - jh < w, 1.0, 0.0).astype(BF16)

    pos = i * tm + lax.broadcasted_iota(jnp.int32, (tm, 1), 0)
    u_all = pin_ref[...]
    halo = halo_ref[...]
    mixed = []
    for gi, w in enumerate(POOL_WINDOWS):
        cs = slice(gi * POOL_GROUP_DIM, (gi + 1) * POOL_GROUP_DIM)
        u = u_all[:, cs]
        wsum = jnp.dot(win_ref[gi], u, preferred_element_type=F32)
        wsum += jnp.dot(hwin_ref[gi], halo[:, cs], preferred_element_type=F32)
        inv_cnt = jnp.where(pos + 1 >= w, 1.0 / w, 1.0 / jnp.minimum(pos + 1, w).astype(F32))
        pooled = wsum * inv_cnt - u.astype(F32)
        mixed.append(jnp.dot(pooled.astype(BF16), pw_ref[gi], preferred_element_type=F32))
    mixed = jnp.concatenate(mixed, axis=1) * ps_ref[...]
    halo_ref[...] = u_all[tm - MAX_HALO:, :]

    y_pool = jnp.dot(mixed.astype(BF16), wpp_ref[...], preferred_element_type=F32)
    y_attn = jnp.dot(attn_ref[...], wap_ref[...], preferred_element_type=F32)
    gates = jax.nn.sigmoid(gl_ref[...].astype(F32) + bg_ref[...])
    merged = gates[:, :D_MODEL] * y_attn + gates[:, D_MODEL:] * y_pool
    o_ref[...] = merged.astype(o_ref.dtype)


def _mix(attn, proj, b_gate, wap_bf, wpp_bf, pw_bf, pool_scale):
    s = attn.shape[0]
    tm = TM_MIX
    pin_col = 3 * ATTN_WIDTH // POOL_WIDTH
    gl_col = (3 * ATTN_WIDTH + POOL_WIDTH) // (N_BRANCHES * D_MODEL)
    return pl.pallas_call(
        _mix_kernel,
        grid=(s // tm,),
        in_specs=[
            pl.BlockSpec((tm, ATTN_WIDTH), lambda i: (i, 0)),
            pl.BlockSpec((tm, POOL_WIDTH), lambda i: (i, pin_col)),
            pl.BlockSpec((tm, N_BRANCHES * D_MODEL), lambda i: (i, gl_col)),
            pl.BlockSpec((1, N_BRANCHES * D_MODEL), lambda i: (0, 0)),
            _resident((ATTN_WIDTH, D_MODEL), lambda i: (0, 0)),
            _resident((POOL_WIDTH, D_MODEL), lambda i: (0, 0)),
            _resident((POOL_GROUPS, POOL_GROUP_DIM, POOL_GROUP_DIM), lambda i: (0, 0, 0)),
            pl.BlockSpec((1, POOL_WIDTH), lambda i: (0, 0)),
        ],
        out_specs=pl.BlockSpec((tm, D_MODEL), lambda i: (i, 0)),
        out_shape=jax.ShapeDtypeStruct((s, D_MODEL), BF16),
        scratch_shapes=[
            pltpu.VMEM((MAX_HALO, POOL_WIDTH), BF16),
            pltpu.VMEM((POOL_GROUPS, tm, tm), BF16),
            pltpu.VMEM((POOL_GROUPS, tm, MAX_HALO), BF16),
        ],
        compiler_params=_cparams(("arbitrary",), 48),
        name="mix",
    )(attn, proj, proj, b_gate, wap_bf, wpp_bf, pw_bf, pool_scale)


META_E, META_POS, META_W = 0, TOP_K, 2 * TOP_K


def _post_kernel(x_ref, m_ref, wout_ref, g_ref, rwh_ref, rwl_ref, rb_ref,
                 x1_ref, h_ref, meta_ref, cnt_ref, carry_ref, tri_ref):
    i = pl.program_id(0)
    tm = TM_POST

    @pl.when(i == 0)
    def _():
        carry_ref[...] = jnp.zeros_like(carry_ref)
        t = lax.broadcasted_iota(jnp.int32, (tm, tm), 0)
        j = lax.broadcasted_iota(jnp.int32, (tm, tm), 1)
        tri_ref[...] = jnp.where(j < t, 1.0, 0.0).astype(BF16)

    x1 = x_ref[...] + jnp.dot(m_ref[...], wout_ref[...], preferred_element_type=F32)
    x1_ref[...] = x1
    var = jnp.mean(x1 * x1, axis=-1, keepdims=True)
    h = x1 * lax.rsqrt(var + RMS_EPS) * g_ref[...]
    h_ref[...] = h

    h_hi = h.astype(BF16)
    h_lo = (h - h_hi.astype(F32)).astype(BF16)
    logits = (jnp.dot(h_hi, rwh_ref[...], preferred_element_type=F32)
              + jnp.dot(h_hi, rwl_ref[...], preferred_element_type=F32)
              + jnp.dot(h_lo, rwh_ref[...], preferred_element_type=F32)
              + rb_ref[...])

    lane_e = lax.broadcasted_iota(jnp.int32, (tm, N_EXPERTS), 1).astype(F32)
    work = logits
    sel = jnp.zeros((tm, N_EXPERTS), F32)
    picks = []
    for _ in range(TOP_K):
        m = jnp.max(work, axis=-1, keepdims=True)
        idx = jnp.min(jnp.where(work == m, lane_e, float(N_EXPERTS)), axis=-1, keepdims=True)
        onehot = lane_e == idx
        picks.append((m, idx, onehot))
        sel = jnp.where(onehot, 1.0, sel)
        work = jnp.where(onehot, -jnp.inf, work)

    rank = jnp.dot(tri_ref[...], sel.astype(BF16), preferred_element_type=F32) + carry_ref[...]
    carry_ref[...] += jnp.sum(sel, axis=0, keepdims=True)
    cnt_ref[...] = carry_ref[...]

    top = picks[0][0]
    exps = [jnp.exp(m - top) for m, _, _ in picks]
    denom = exps[0] + exps[1] + exps[2] + exps[3]
    lane = lax.broadcasted_iota(jnp.int32, (tm, LANES), 1)
    meta = jnp.zeros((tm, LANES), F32)
    for k, (m, idx, onehot) in enumerate(picks):
        pos_k = jnp.sum(jnp.where(onehot, rank, 0.0), axis=-1, keepdims=True)
        meta = jnp.where(lane == META_E + k, idx, meta)
        meta = jnp.where(lane == META_POS + k, pos_k, meta)
        meta = jnp.where(lane == META_W + k, exps[k] / denom, meta)
    meta_ref[...] = meta


def _post(x2, merged, wout_bf, g, rw_hi, rw_lo, rb):
    s = x2.shape[0]
    tm = TM_POST
    return pl.pallas_call(
        _post_kernel,
        grid=(s // tm,),
        in_specs=[
            pl.BlockSpec((tm, D_MODEL), lambda i: (i, 0)),
            pl.BlockSpec((tm, D_MODEL), lambda i: (i, 0)),
            _resident((D_MODEL, D_MODEL), lambda i: (0, 0)),
            pl.BlockSpec((1, D_MODEL), lambda i: (0, 0)),
            pl.BlockSpec((D_MODEL, N_EXPERTS), lambda i: (0, 0)),
            pl.BlockSpec((D_MODEL, N_EXPERTS), lambda i: (0, 0)),
            pl.BlockSpec((1, N_EXPERTS), lambda i: (0, 0)),
        ],
        out_specs=[
            pl.BlockSpec((tm, D_MODEL), lambda i: (i, 0)),
            pl.BlockSpec((tm, D_MODEL), lambda i: (i, 0)),
            pl.BlockSpec((tm, LANES), lambda i: (i, 0)),
            pl.BlockSpec((1, N_EXPERTS), lambda i: (0, 0)),
        ],
        out_shape=[
            jax.ShapeDtypeStruct((s, D_MODEL), F32),
            jax.ShapeDtypeStruct((s, D_MODEL), F32),
            jax.ShapeDtypeStruct((s, LANES), F32),
            jax.ShapeDtypeStruct((1, N_EXPERTS), F32),
        ],
        scratch_shapes=[
            pltpu.VMEM((1, N_EXPERTS), F32),
            pltpu.VMEM((tm, tm), BF16),
        ],
        compiler_params=_cparams(("arbitrary",), 48),
        name="post",
    )(x2, merged, wout_bf, g, rw_hi, rw_lo, rb)


def _row_copy(src, src_row, dst, dst_row, sem):
    return pltpu.make_async_copy(src.at[pl.ds(src_row, 1)], dst.at[pl.ds(dst_row, 1)], sem)


def _rows_copy(src, dst, dst_row, n_rows, sem):
    return pltpu.make_async_copy(src.at[pl.ds(0, n_rows)],
                                 dst.at[pl.ds(pl.multiple_of(dst_row, SUB_BLOCK), n_rows)], sem)


def _dispatch_kernel(padbase_ref, padcnt_ref, subbase_ref, subcnt_ref, nreal_ref,
                     dest_ref, h_ref, zero_hbm, xs_hbm, sem, pad_sem):
    i = pl.program_id(0)
    tm = TM_DISP
    n_blocks = xs_hbm.shape[0] // EXPERT_BLOCK

    for t in range(tm):
        for k in range(TOP_K):
            _row_copy(h_ref, t, xs_hbm, dest_ref[0, 0, TOP_K * t + k], sem).start(priority=k % 2)

    @pl.when(i == 0)
    def _():
        for e in range(N_EXPERTS):
            def fill(r, c, e=e):
                _row_copy(zero_hbm, 0, xs_hbm, padbase_ref[e] + r, pad_sem).start()
                return c
            lax.fori_loop(0, padcnt_ref[e], fill, 0)

            def fill_sub(r, c, e=e):
                _rows_copy(zero_hbm, xs_hbm, subbase_ref[e] + r * SUB_BLOCK, SUB_BLOCK,
                           pad_sem).start()
                return c
            lax.fori_loop(0, subcnt_ref[e], fill_sub, 0)

        def fill_block(b, c):
            _rows_copy(zero_hbm, xs_hbm, b * EXPERT_BLOCK, EXPERT_BLOCK, pad_sem).start()
            return c
        lax.fori_loop(nreal_ref[0], n_blocks, fill_block, 0)

        for e in range(N_EXPERTS):
            def drain(r, c):
                _row_copy(zero_hbm, 0, xs_hbm, 0, pad_sem).wait()
                return c
            lax.fori_loop(0, padcnt_ref[e], drain, 0)

            def drain_sub(r, c):
                _rows_copy(zero_hbm, xs_hbm, 0, SUB_BLOCK, pad_sem).wait()
                return c
            lax.fori_loop(0, subcnt_ref[e], drain_sub, 0)

        def drain_block(b, c):
            _rows_copy(zero_hbm, xs_hbm, 0, EXPERT_BLOCK, pad_sem).wait()
            return c
        lax.fori_loop(nreal_ref[0], n_blocks, drain_block, 0)

    for _ in range(TOP_K * tm // DRAIN_ROWS):
        pltpu.make_async_copy(h_ref.at[pl.ds(0, DRAIN_ROWS)], xs_hbm.at[pl.ds(0, DRAIN_ROWS)],
                              sem).wait()


def _dispatch(fills, nreal, dest, h, n_rows):
    s = h.shape[0]
    tm = TM_DISP
    dest3 = dest.reshape(s // tm, 1, TOP_K * tm)
    zero_rows = jnp.zeros((EXPERT_BLOCK, D_MODEL), F32)
    grid_spec = pltpu.PrefetchScalarGridSpec(
        num_scalar_prefetch=5,
        grid=(s // tm,),
        in_specs=[
            pl.BlockSpec((1, 1, TOP_K * tm), lambda i, *_: (i, 0, 0), memory_space=pltpu.SMEM),
            pl.BlockSpec((tm, D_MODEL), lambda i, *_: (i, 0)),
            pl.BlockSpec(memory_space=pl.ANY),
        ],
        out_specs=pl.BlockSpec(memory_space=pl.ANY),
        scratch_shapes=[pltpu.SemaphoreType.DMA, pltpu.SemaphoreType.DMA],
    )
    return pl.pallas_call(
        _dispatch_kernel,
        grid_spec=grid_spec,
        out_shape=jax.ShapeDtypeStruct((n_rows, D_MODEL), F32),
        compiler_params=_cparams(("arbitrary",), 32),
        name="dispatch",
    )(*fills, nreal, dest3, h, zero_rows)


def _per_sub_block(nsub, o_ref, compute):
    @pl.when(nsub == SUBS_PER_BLOCK)
    def _():
        compute(slice(0, EXPERT_BLOCK))

    for n in range(SUBS_PER_BLOCK):
        @pl.when(nsub == n)
        def _(n=n):
            if n:
                compute(slice(0, n * SUB_BLOCK))
            o_ref[n * SUB_BLOCK:, :] = jnp.zeros((EXPERT_BLOCK - n * SUB_BLOCK, o_ref.shape[1]),
                                                 o_ref.dtype)


def _up_kernel(be_ref, nsub_ref, nreal_ref, nexte_ref, x_ref, w_hbm, bg_ref, bu_ref, perm_ref,
               a_ref, stage_ref, wg_ref, wu_ref, sem):
    half = pl.program_id(0)
    b = pl.program_id(1)
    e = be_ref[b]

    def fetch(expert, hf):
        cols = pl.ds(pl.multiple_of(hf * UP_HALF, UP_HALF), UP_HALF)
        return pltpu.make_async_copy(w_hbm.at[expert, :, cols], stage_ref, sem)

    @pl.when((half == 0) & (b == 0))
    def _():
        fetch(e, 0).start()

    @pl.when((b == 0) | (e != be_ref[jnp.maximum(b - 1, 0)]))
    def _():
        fetch(e, half).wait()
        for t in range(UP_HALF // SPLIT_TILE):
            w = stage_ref[:, t * SPLIT_TILE:(t + 1) * SPLIT_TILE].astype(BF16)
            r = jnp.dot(w, perm_ref[...], preferred_element_type=F32)
            cs = slice(t * (SPLIT_TILE // 2), (t + 1) * (SPLIT_TILE // 2))
            wg_ref[:, cs] = r[:, :SPLIT_TILE // 2].astype(BF16)
            wu_ref[:, cs] = r[:, SPLIT_TILE // 2:].astype(BF16)
        last_run = nexte_ref[b] == e

        @pl.when(jnp.logical_not(last_run))
        def _():
            fetch(nexte_ref[b], half).start()

        @pl.when(last_run & (half == 0))
        def _():
            fetch(be_ref[0], 1).start()

    def swiglu(rows):
        x = x_ref[rows, :].astype(BF16)
        for c in range(UP_HALF // 2 // UP_COLS):
            cs = slice(c * UP_COLS, (c + 1) * UP_COLS)
            g = jnp.dot(x, wg_ref[:, cs], preferred_element_type=F32) + bg_ref[0, :, cs]
            u = jnp.dot(x, wu_ref[:, cs], preferred_element_type=F32) + bu_ref[0, :, cs]
            g = jnp.minimum(g, SWIGLU_LIMIT)
            u = jnp.clip(u, -SWIGLU_LIMIT, SWIGLU_LIMIT)
            a_ref[rows, cs] = (g * jax.nn.sigmoid(SWIGLU_ALPHA * g) * (u + 1.0)).astype(a_ref.dtype)

    _per_sub_block(nsub_ref[b], a_ref, swiglu)


def _down_kernel(be_ref, nsub_ref, nreal_ref, nexte_ref, a_ref, w_hbm, b_ref, y_ref,
                 stage_ref, wbf_ref, sem):
    b = pl.program_id(0)
    e = be_ref[b]

    def fetch(expert):
        return pltpu.make_async_copy(w_hbm.at[expert], stage_ref, sem)

    @pl.when(b == 0)
    def _():
        fetch(e).start()

    @pl.when((b == 0) | (e != be_ref[jnp.maximum(b - 1, 0)]))
    def _():
        fetch(e).wait()
        wbf_ref[...] = stage_ref[...].astype(BF16)

        @pl.when(nexte_ref[b] != e)
        def _():
            fetch(nexte_ref[b]).start()

    def project(rows):
        y_ref[rows, :] = jnp.dot(a_ref[rows, :], wbf_ref[...], preferred_element_type=F32) + b_ref[0]

    _per_sub_block(nsub_ref[b], y_ref, project)


def _row_block(b, be, ns, nr, *_):
    return (jnp.minimum(b, nr[0] - 1), 0)


def _expert_block(b, be, *_):
    return (be[b], 0, 0)


def _experts(block_e, nsub, nreal, next_e, xs, w1, b1g, b1u, w2, b2):
    n_rows = xs.shape[0]
    nb = n_rows // EXPERT_BLOCK
    i = jnp.arange(SPLIT_TILE)[:, None]
    c = jnp.arange(SPLIT_TILE)[None, :]
    perm = jnp.where(c < SPLIT_TILE // 2, i == 2 * c, i == 2 * (c - SPLIT_TILE // 2) + 1).astype(BF16)
    act = pl.pallas_call(
        _up_kernel,
        grid_spec=pltpu.PrefetchScalarGridSpec(
            num_scalar_prefetch=4,
            grid=(2 * D_FF // UP_HALF, nb),
            in_specs=[
                pl.BlockSpec((EXPERT_BLOCK, D_MODEL),
                             lambda hf, b, be, ns, nr, ne: (jnp.minimum(b, nr[0] - 1), 0)),
                pl.BlockSpec(memory_space=pl.ANY),
                pl.BlockSpec((1, 1, UP_HALF // 2), lambda hf, b, be, *_: (be[b], 0, hf)),
                pl.BlockSpec((1, 1, UP_HALF // 2), lambda hf, b, be, *_: (be[b], 0, hf)),
                pl.BlockSpec((SPLIT_TILE, SPLIT_TILE), lambda hf, b, *_: (0, 0)),
            ],
            out_specs=pl.BlockSpec((EXPERT_BLOCK, UP_HALF // 2), lambda hf, b, *_: (b, hf)),
            scratch_shapes=[
                pltpu.VMEM((D_MODEL, UP_HALF), F32),
                pltpu.VMEM((D_MODEL, UP_HALF // 2), BF16),
                pltpu.VMEM((D_MODEL, UP_HALF // 2), BF16),
                pltpu.SemaphoreType.DMA,
            ],
        ),
        out_shape=jax.ShapeDtypeStruct((n_rows, D_FF), BF16),
        compiler_params=_cparams(("arbitrary", "arbitrary"), 56),
        name="expert_up",
    )(block_e, nsub, nreal, next_e, xs, w1, b1g, b1u, perm)
    return pl.pallas_call(
        _down_kernel,
        grid_spec=pltpu.PrefetchScalarGridSpec(
            num_scalar_prefetch=4,
            grid=(nb,),
            in_specs=[
                pl.BlockSpec((EXPERT_BLOCK, D_FF), _row_block),
                pl.BlockSpec(memory_space=pl.ANY),
                pl.BlockSpec((1, 1, D_MODEL), _expert_block),
            ],
            out_specs=pl.BlockSpec((EXPERT_BLOCK, D_MODEL), lambda b, *_: (b, 0)),
            scratch_shapes=[
                pltpu.VMEM((D_FF, D_MODEL), F32),
                pltpu.VMEM((D_FF, D_MODEL), BF16),
                pltpu.SemaphoreType.DMA,
            ],
        ),
        out_shape=jax.ShapeDtypeStruct((n_rows, D_MODEL), F32),
        compiler_params=_cparams(("arbitrary",), 56),
        name="expert_down",
    )(block_e, nsub, nreal, next_e, act, w2, b2)


def _combine_kernel(dcur_ref, dnxt_ref, x1_ref, meta_ref, g_ref, y_hbm, o_ref, ybuf, sem):
    i = pl.program_id(0)
    n = pl.num_programs(0)
    tm = TM_COMB
    slot = i % 2

    def row_gather(dref, s, t, k):
        return pltpu.make_async_copy(y_hbm.at[pl.ds(dref[0, 0, TOP_K * t + k], 1)],
                                     ybuf.at[s, k, pl.ds(t, 1)], sem.at[s])

    @pl.when(i == 0)
    def _():
        def body(t, c):
            for k in range(TOP_K):
                row_gather(dcur_ref, 0, t, k).start()
            return c
        lax.fori_loop(0, tm, body, 0, unroll=8)

    for s in range(2):
        @pl.when((i + 1 < n) & (slot == 1 - s))
        def _(s=s):
            for t in range(tm):
                for k in range(TOP_K):
                    row_gather(dnxt_ref, s, t, k).start(priority=k % 2)

    for k in range(TOP_K):
        pltpu.make_async_copy(y_hbm.at[pl.ds(0, tm)], ybuf.at[slot, k], sem.at[slot]).wait()

    meta = meta_ref[...]
    acc = x1_ref[...]
    for k in range(TOP_K):
        acc = acc + meta[:, META_W + k:META_W + k + 1] * ybuf[slot, k]
    var = jnp.mean(acc * acc, axis=-1, keepdims=True)
    o_ref[...] = (acc * lax.rsqrt(var + RMS_EPS) * g_ref[...]).astype(o_ref.dtype)


def _combine(dest, x1, meta, g, y):
    s = x1.shape[0]
    tm = TM_COMB
    nt = s // tm
    dest3 = dest.reshape(nt, 1, TOP_K * tm)
    return pl.pallas_call(
        _combine_kernel,
        grid=(nt,),
        in_specs=[
            pl.BlockSpec((1, 1, TOP_K * tm), lambda i: (i, 0, 0), memory_space=pltpu.SMEM),
            pl.BlockSpec((1, 1, TOP_K * tm), lambda i: (jnp.minimum(i + 1, nt - 1), 0, 0),
                         memory_space=pltpu.SMEM),
            pl.BlockSpec((tm, D_MODEL), lambda i: (i, 0)),
            pl.BlockSpec((tm, LANES), lambda i: (i, 0)),
            pl.BlockSpec((1, D_MODEL), lambda i: (0, 0)),
            pl.BlockSpec(memory_space=pl.ANY),
        ],
        out_specs=pl.BlockSpec((tm, D_MODEL), lambda i: (i, 0)),
        out_shape=jax.ShapeDtypeStruct((s, D_MODEL), F32),
        scratch_shapes=[
            pltpu.VMEM((2, TOP_K, tm, D_MODEL), F32),
            pltpu.SemaphoreType.DMA((2,)),
        ],
        compiler_params=_cparams(("arbitrary",), 40),
        name="combine",
    )(dest3, dest3, x1, meta, g, y)


def _routing_tables(meta, cnt, n_blocks):
    counts = cnt[0].astype(jnp.int32)
    padded = ((counts + EXPERT_BLOCK - 1) // EXPERT_BLOCK) * EXPERT_BLOCK
    pad_end = jnp.cumsum(padded)
    pad_start = pad_end - padded
    e4 = meta[:, META_E:META_E + TOP_K].astype(jnp.int32)
    pos4 = meta[:, META_POS:META_POS + TOP_K].astype(jnp.int32)
    experts = jnp.arange(N_EXPERTS, dtype=jnp.int32)
    dest = pos4 + jnp.sum(jnp.where(e4[..., None] == experts, pad_start, 0), axis=-1)
    nreal = pad_end[-1:] // EXPERT_BLOCK
    blk_raw = jnp.arange(n_blocks, dtype=jnp.int32)
    blk = jnp.minimum(blk_raw, nreal[0] - 1)
    block_e = jnp.sum(pad_end[None, :] <= (blk * EXPERT_BLOCK)[:, None], axis=1).astype(jnp.int32)
    onehot_e = block_e[:, None] == experts
    pick = lambda table: jnp.sum(jnp.where(onehot_e, table, 0), axis=1)
    rows_left = pick(counts) - (blk * EXPERT_BLOCK - pick(pad_start))
    nsub = jnp.clip((rows_left + SUB_BLOCK - 1) // SUB_BLOCK, 0, SUBS_PER_BLOCK)
    nsub = jnp.where(blk_raw < nreal[0], nsub, 0)
    later = experts[None, :] > block_e[:, None]
    next_e = jnp.min(jnp.where(later & (counts > 0)[None, :], experts, N_EXPERTS), axis=1)
    next_e = jnp.where(next_e == N_EXPERTS, block_e, next_e)
    sub_rows = ((counts + SUB_BLOCK - 1) // SUB_BLOCK) * SUB_BLOCK
    fills = (pad_start + counts, sub_rows - counts, pad_start + sub_rows,
             (padded - sub_rows) // SUB_BLOCK)
    i32 = lambda v: v.astype(jnp.int32)
    return dest, i32(block_e), i32(nsub), i32(nreal), i32(next_e), tuple(i32(f) for f in fills)


def kernel(x, norm_mix, w_in, b_gate, rel_bias, w_attn_proj, pool_w, pool_scale, w_pool_proj,
           w_out, norm_ffn, router_w, router_b, w1, b1, w2, b2, norm_final):
    bsz, seq, d = x.shape
    n_tok = bsz * seq
    assert w_in.shape[0] == 1 and bsz == 1 and d == D_MODEL and seq % TM_IN == 0
    n_blocks = -(-(n_tok * TOP_K) // EXPERT_BLOCK) + N_EXPERTS
    n_rows = n_blocks * EXPERT_BLOCK
    x2 = x.reshape(n_tok, d)
    row = lambda v: v.reshape(1, -1).astype(F32)
    l = 0
    proj = _inproj(x2, row(norm_mix[l]), w_in[l].astype(BF16))
    attn = _attention(proj, _attn_bias_table(rel_bias[l]))
    merged = _mix(attn, proj, row(b_gate[l]), w_attn_proj[l].astype(BF16),
                  w_pool_proj[l].astype(BF16), pool_w[l].astype(BF16), row(pool_scale[l]))
    rw = router_w[l].astype(F32)
    rw_hi = rw.astype(BF16)
    rw_lo = (rw - rw_hi.astype(F32)).astype(BF16)
    x1, h, meta, cnt = _post(x2, merged, w_out[l].astype(BF16), row(norm_ffn[l]),
                             rw_hi, rw_lo, row(router_b[l]))
    dest, block_e, nsub, nreal, next_e, fills = _routing_tables(meta, cnt, n_blocks)
    xs = _dispatch(fills, nreal, dest, h, n_rows)
    y = _experts(block_e, nsub, nreal, next_e, xs, w1[l],
                 b1[l][:, None, 0::2].astype(F32), b1[l][:, None, 1::2].astype(F32),
                 w2[l], b2[l][:, None, :].astype(F32))
    out = _combine(dest, x1, meta, row(norm_final), y)
    return out.reshape(bsz, seq, d)
```

```python
import jax
import jax.numpy as jnp
from jax import lax
from jax.experimental import pallas as pl
from jax.experimental.pallas import tpu as pltpu

D_MODEL = 2048
CHUNK = 64
LEFT_CHUNKS = 8
BAND = (LEFT_CHUNKS + 1) * CHUNK
ATTN_WIDTH = D_MODEL // 2
HEAD_DIM = 64
ATTN_HEADS = ATTN_WIDTH // HEAD_DIM
MAX_REL = 256
POOL_WINDOWS = (2, 4, 8, 16)
POOL_GROUPS = len(POOL_WINDOWS)
POOL_WIDTH = D_MODEL // 2
POOL_GROUP_DIM = POOL_WIDTH // POOL_GROUPS
N_BRANCHES = 2
IN_WIDTH = 3 * ATTN_WIDTH + POOL_WIDTH + N_BRANCHES * D_MODEL
N_EXPERTS = 32
TOP_K = 4
D_FF = D_MODEL
SWIGLU_LIMIT = 7.0
SWIGLU_ALPHA = 1.702
EXPERT_BLOCK = 512
SUB_BLOCK = 256
SUBS_PER_BLOCK = EXPERT_BLOCK // SUB_BLOCK
UP_HALF = D_FF
SPLIT_TILE = 256
UP_COLS = 512
RMS_EPS = 1e-5
NEG_INF = -1e30
LOG2_E = 1.4426950408889634

LANES = 128
MAX_HALO = max(POOL_WINDOWS)

TM_IN, TN_IN = 1024, 2048
Q_GROUP = 4
TQ = Q_GROUP * CHUNK
KV_BLOCKS = LEFT_CHUNKS // Q_GROUP + 1
KV_SPAN = KV_BLOCKS * TQ
TM_MIX = 512
TM_POST = 512
TM_DISP = 1024
DRAIN_ROWS = 512
TM_COMB = 256

MIB = 1024 * 1024
F32 = jnp.float32
BF16 = jnp.bfloat16


def _cparams(sem, vmem_mib):
    return pltpu.CompilerParams(dimension_semantics=sem, vmem_limit_bytes=vmem_mib * MIB)


def _resident(shape, index_map):
    return pl.BlockSpec(shape, index_map, pipeline_mode=pl.Buffered(1))


def _inproj_kernel(x_ref, g_ref, w_ref, o_ref, xn_ref):
    @pl.when(pl.program_id(1) == 0)
    def _():
        x = x_ref[...]
        var = jnp.mean(x * x, axis=-1, keepdims=True)
        xn_ref[...] = (x * lax.rsqrt(var + RMS_EPS) * g_ref[...]).astype(BF16)

    o_ref[...] = jnp.dot(xn_ref[...], w_ref[...], preferred_element_type=F32).astype(o_ref.dtype)


def _inproj(x2, g, w_bf):
    s = x2.shape[0]
    return pl.pallas_call(
        _inproj_kernel,
        grid=(s // TM_IN, IN_WIDTH // TN_IN),
        in_specs=[
            pl.BlockSpec((TM_IN, D_MODEL), lambda i, j: (i, 0)),
            pl.BlockSpec((1, D_MODEL), lambda i, j: (0, 0)),
            pl.BlockSpec((D_MODEL, TN_IN), lambda i, j: (0, j)),
        ],
        out_specs=pl.BlockSpec((TM_IN, TN_IN), lambda i, j: (i, j)),
        out_shape=jax.ShapeDtypeStruct((s, IN_WIDTH), BF16),
        scratch_shapes=[pltpu.VMEM((TM_IN, D_MODEL), BF16)],
        compiler_params=_cparams(("parallel", "arbitrary"), 56),
        name="inproj",
    )(x2, g, w_bf)


def _attn_heads(q_ref, k_refs, v_refs, bias_ref, o_ref, valid):
    lane = lax.broadcasted_iota(jnp.int32, (1, LANES), 1)
    scale = HEAD_DIM ** -0.5 * LOG2_E
    nt = (((1,), (1,)), ((), ()))
    for hp in range(ATTN_HEADS // 2):
        cs = slice(hp * LANES, (hp + 1) * LANES)
        qp = q_ref[:, cs]
        ks = [r[:, cs] for r in k_refs]
        vs = [r[:, cs] for r in v_refs]
        zero = jnp.zeros_like(qp)
        q2 = jnp.concatenate([jnp.where(lane < HEAD_DIM, qp, zero),
                              jnp.where(lane >= HEAD_DIM, qp, zero)], axis=0)
        s = jnp.concatenate(
            [lax.dot_general(q2, kb, nt, preferred_element_type=F32) for kb in ks], axis=1)
        s = s * scale + bias_ref[hp]
        if valid is not None:
            s = jnp.where(valid, s, NEG_INF)
        m = jnp.max(s, axis=-1, keepdims=True)
        p = jnp.exp2(s - m)
        l = jnp.sum(p, axis=-1, keepdims=True)
        pb = p.astype(BF16)
        o = jnp.dot(pb[:, :TQ], vs[0], preferred_element_type=F32)
        for n in range(1, KV_BLOCKS):
            o += jnp.dot(pb[:, n * TQ:(n + 1) * TQ], vs[n], preferred_element_type=F32)
        o = o / l
        o_ref[:, cs] = jnp.where(lane < HEAD_DIM, o[:TQ], o[TQ:]).astype(o_ref.dtype)


def _attn_kernel(q_ref, *refs):
    i = pl.program_id(0)
    k_refs, v_refs = refs[:KV_BLOCKS], refs[KV_BLOCKS:2 * KV_BLOCKS]
    bias_ref, o_ref = refs[2 * KV_BLOCKS:]
    first_full = LEFT_CHUNKS * CHUNK // TQ

    @pl.when(i < first_full)
    def _():
        col = lax.broadcasted_iota(jnp.int32, (1, KV_SPAN), 1)
        _attn_heads(q_ref, k_refs, v_refs, bias_ref, o_ref, col >= (LEFT_CHUNKS * CHUNK - TQ * i))

    @pl.when(i >= first_full)
    def _():
        _attn_heads(q_ref, k_refs, v_refs, bias_ref, o_ref, None)


def _attn_bias_table(rel_bias):
    span = BAND + CHUNK - 1
    dist = LEFT_CHUNKS * CHUNK + (CHUNK - 1) - jnp.arange(span)
    by_col = rel_bias.astype(F32)[:, jnp.clip(dist, -(CHUNK - 1), MAX_REL) + (CHUNK - 1)]
    skew = jnp.tile(by_col, (1, CHUNK + 1))[:, :CHUNK * (span + 1)].reshape(-1, CHUNK, span + 1)
    band_bias = skew[:, ::-1, :BAND]
    rows = [
        jnp.pad(band_bias, ((0, 0), (0, 0), (c * CHUNK, KV_SPAN - BAND - c * CHUNK)),
                constant_values=NEG_INF)
        for c in range(Q_GROUP)
    ]
    table = jnp.concatenate(rows, axis=1) * LOG2_E
    return table.reshape(ATTN_HEADS // 2, 2 * TQ, KV_SPAN)


def _attention(proj, bias_tab):
    s = proj.shape[0]
    kcol, vcol = 1, 2

    def kv_spec(back, colblk):
        return pl.BlockSpec((TQ, ATTN_WIDTH), lambda i: (jnp.maximum(i - back, 0), colblk))

    return pl.pallas_call(
        _attn_kernel,
        grid=(s // TQ,),
        in_specs=[
            pl.BlockSpec((TQ, ATTN_WIDTH), lambda i: (i, 0)),
            *[kv_spec(back, kcol) for back in reversed(range(KV_BLOCKS))],
            *[kv_spec(back, vcol) for back in reversed(range(KV_BLOCKS))],
            _resident((ATTN_HEADS // 2, 2 * TQ, KV_SPAN), lambda i: (0, 0, 0)),
        ],
        out_specs=pl.BlockSpec((TQ, ATTN_WIDTH), lambda i: (i, 0)),
        out_shape=jax.ShapeDtypeStruct((s, ATTN_WIDTH), BF16),
        compiler_params=_cparams(("parallel",), 48),
        name="attn",
    )(*([proj] * (1 + 2 * KV_BLOCKS)), bias_tab)


def _mix_kernel(attn_ref, pin_ref, gl_ref, bg_ref, wap_ref, wpp_ref, pw_ref, ps_ref,
                o_ref, halo_ref, win_ref, hwin_ref):
    i = pl.program_id(0)
    tm = TM_MIX

    @pl.when(i == 0)
    def _():
        halo_ref[...] = jnp.zeros_like(halo_ref)
        t = lax.broadcasted_iota(jnp.int32, (tm, tm), 0)
        j = lax.broadcasted_iota(jnp.int32, (tm, tm), 1)
        th = lax.broadcasted_iota(jnp.int32, (tm, MAX_HALO), 0)
        jh = lax.broadcasted_iota(jnp.int32, (tm, MAX_HALO), 1)
        for gi, w in enumerate(POOL_WINDOWS):
            win_ref[gi] = jnp.where((t - j >= 0) & (t - j < w), 1.0, 0.0).astype(BF16)
            hwin_ref[gi] = jnp.where(th + MAX_HALO - jh < w, 1.0, 0.0).astype(BF16)

    pos = i * tm + lax.broadcasted_iota(jnp.int32, (tm, 1), 0)
    u_all = pin_ref[...]
    halo = halo_ref[...]
    mixed = []
    for gi, w in enumerate(POOL_WINDOWS):
        cs = slice(gi * POOL_GROUP_DIM, (gi + 1) * POOL_GROUP_DIM)
        u = u_all[:, cs]
        wsum = jnp.dot(win_ref[gi], u, preferred_element_type=F32)
        wsum += jnp.dot(hwin_ref[gi], halo[:, cs], preferred_element_type=F32)
        inv_cnt = jnp.where(pos + 1 >= w, 1.0 / w, 1.0 / jnp.minimum(pos + 1, w).astype(F32))
        pooled = wsum * inv_cnt - u.astype(F32)
        mixed.append(jnp.dot(pooled.astype(BF16), pw_ref[gi], preferred_element_type=F32))
    mixed = jnp.concatenate(mixed, axis=1) * ps_ref[...]
    halo_ref[...] = u_all[tm - MAX_HALO:, :]

    y_pool = jnp.dot(mixed.astype(BF16), wpp_ref[...], preferred_element_type=F32)
    y_attn = jnp.dot(attn_ref[...], wap_ref[...], preferred_element_type=F32)
    gates = jax.nn.sigmoid(gl_ref[...].astype(F32) + bg_ref[...])
    merged = gates[:, :D_MODEL] * y_attn + gates[:, D_MODEL:] * y_pool
    o_ref[...] = merged.astype(o_ref.dtype)


def _mix(attn, proj, b_gate, wap_bf, wpp_bf, pw_bf, pool_scale):
    s = attn.shape[0]
    tm = TM_MIX
    pin_col = 3 * ATTN_WIDTH // POOL_WIDTH
    gl_col = (3 * ATTN_WIDTH + POOL_WIDTH) // (N_BRANCHES * D_MODEL)
    return pl.pallas_call(
        _mix_kernel,
        grid=(s // tm,),
        in_specs=[
            pl.BlockSpec((tm, ATTN_WIDTH), lambda i: (i, 0)),
            pl.BlockSpec((tm, POOL_WIDTH), lambda i: (i, pin_col)),
            pl.BlockSpec((tm, N_BRANCHES * D_MODEL), lambda i: (i, gl_col)),
            pl.BlockSpec((1, N_BRANCHES * D_MODEL), lambda i: (0, 0)),
            _resident((ATTN_WIDTH, D_MODEL), lambda i: (0, 0)),
            _resident((POOL_WIDTH, D_MODEL), lambda i: (0, 0)),
            _resident((POOL_GROUPS, POOL_GROUP_DIM, POOL_GROUP_DIM), lambda i: (0, 0, 0)),
            pl.BlockSpec((1, POOL_WIDTH), lambda i: (0, 0)),
        ],
        out_specs=pl.BlockSpec((tm, D_MODEL), lambda i: (i, 0)),
        out_shape=jax.ShapeDtypeStruct((s, D_MODEL), BF16),
        scratch_shapes=[
            pltpu.VMEM((MAX_HALO, POOL_WIDTH), BF16),
            pltpu.VMEM((POOL_GROUPS, tm, tm), BF16),
            pltpu.VMEM((POOL_GROUPS, tm, MAX_HALO), BF16),
        ],
        compiler_params=_cparams(("arbitrary",), 48),
        name="mix",
    )(attn, proj, proj, b_gate, wap_bf, wpp_bf, pw_bf, pool_scale)


META_E, META_POS, META_W = 0, TOP_K, 2 * TOP_K


def _post_kernel(x_ref, m_ref, wout_ref, g_ref, rwh_ref, rwl_ref, rb_ref,
                 x1_ref, h_ref, meta_ref, cnt_ref, carry_ref, tri_ref):
    i = pl.program_id(0)
    tm = TM_POST

    @pl.when(i == 0)
    def _():
        carry_ref[...] = jnp.zeros_like(carry_ref)
        t = lax.broadcasted_iota(jnp.int32, (tm, tm), 0)
        j = lax.broadcasted_iota(jnp.int32, (tm, tm), 1)
        tri_ref[...] = jnp.where(j < t, 1.0, 0.0).astype(BF16)

    x1 = x_ref[...] + jnp.dot(m_ref[...], wout_ref[...], preferred_element_type=F32)
    x1_ref[...] = x1
    var = jnp.mean(x1 * x1, axis=-1, keepdims=True)
    h = x1 * lax.rsqrt(var + RMS_EPS) * g_ref[...]
    h_ref[...] = h

    h_hi = h.astype(BF16)
    h_lo = (h - h_hi.astype(F32)).astype(BF16)
    logits = (jnp.dot(h_hi, rwh_ref[...], preferred_element_type=F32)
              + jnp.dot(h_hi, rwl_ref[...], preferred_element_type=F32)
              + jnp.dot(h_lo, rwh_ref[...], preferred_element_type=F32)
              + rb_ref[...])

    lane_e = lax.broadcasted_iota(jnp.int32, (tm, N_EXPERTS), 1).astype(F32)
    work = logits
    sel = jnp.zeros((tm, N_EXPERTS), F32)
    picks = []
    for _ in range(TOP_K):
        m = jnp.max(work, axis=-1, keepdims=True)
        idx = jnp.min(jnp.where(work == m, lane_e, float(N_EXPERTS)), axis=-1, keepdims=True)
        onehot = lane_e == idx
        picks.append((m, idx, onehot))
        sel = jnp.where(onehot, 1.0, sel)
        work = jnp.where(onehot, -jnp.inf, work)

    rank = jnp.dot(tri_ref[...], sel.astype(BF16), preferred_element_type=F32) + carry_ref[...]
    carry_ref[...] += jnp.sum(sel, axis=0, keepdims=True)
    cnt_ref[...] = carry_ref[...]

    top = picks[0][0]
    exps = [jnp.exp(m - top) for m, _, _ in picks]
    denom = exps[0]
    for ex in exps[1:]:
        denom = denom + ex
    lane = lax.broadcasted_iota(jnp.int32, (tm, LANES), 1)
    meta = jnp.zeros((tm, LANES), F32)
    for k, (m, idx, onehot) in enumerate(picks):
        pos_k = jnp.sum(jnp.where(onehot, rank, 0.0), axis=-1, keepdims=True)
        meta = jnp.where(lane == META_E + k, idx, meta)
        meta = jnp.where(lane == META_POS + k, pos_k, meta)
        meta = jnp.where(lane == META_W + k, exps[k] / denom, meta)
    meta_ref[...] = meta


def _post(x2, merged, wout_bf, g, rw_hi, rw_lo, rb):
    s = x2.shape[0]
    tm = TM_POST
    return pl.pallas_call(
        _post_kernel,
        grid=(s // tm,),
        in_specs=[
            pl.BlockSpec((tm, D_MODEL), lambda i: (i, 0)),
            pl.BlockSpec((tm, D_MODEL), lambda i: (i, 0)),
            _resident((D_MODEL, D_MODEL), lambda i: (0, 0)),
            pl.BlockSpec((1, D_MODEL), lambda i: (0, 0)),
            pl.BlockSpec((D_MODEL, N_EXPERTS), lambda i: (0, 0)),
            pl.BlockSpec((D_MODEL, N_EXPERTS), lambda i: (0, 0)),
            pl.BlockSpec((1, N_EXPERTS), lambda i: (0, 0)),
        ],
        out_specs=[
            pl.BlockSpec((tm, D_MODEL), lambda i: (i, 0)),
            pl.BlockSpec((tm, D_MODEL), lambda i: (i, 0)),
            pl.BlockSpec((tm, LANES), lambda i: (i, 0)),
            pl.BlockSpec((1, N_EXPERTS), lambda i: (0, 0)),
        ],
        out_shape=[
            jax.ShapeDtypeStruct((s, D_MODEL), F32),
            jax.ShapeDtypeStruct((s, D_MODEL), F32),
            jax.ShapeDtypeStruct((s, LANES), F32),
            jax.ShapeDtypeStruct((1, N_EXPERTS), F32),
        ],
        scratch_shapes=[
            pltpu.VMEM((1, N_EXPERTS), F32),
            pltpu.VMEM((tm, tm), BF16),
        ],
        compiler_params=_cparams(("arbitrary",), 48),
        name="post",
    )(x2, merged, wout_bf, g, rw_hi, rw_lo, rb)


def _row_copy(src, src_row, dst, dst_row, sem):
    return pltpu.make_async_copy(src.at[pl.ds(src_row, 1)], dst.at[pl.ds(dst_row, 1)], sem)


def _rows_copy(src, dst, dst_row, n_rows, sem):
    return pltpu.make_async_copy(src.at[pl.ds(0, n_rows)],
                                 dst.at[pl.ds(pl.multiple_of(dst_row, SUB_BLOCK), n_rows)], sem)


def _dispatch_kernel(padbase_ref, padcnt_ref, subbase_ref, subcnt_ref, nreal_ref,
                     dest_ref, h_ref, zero_hbm, xs_hbm, sem, pad_sem):
    i = pl.program_id(0)
    tm = TM_DISP
    n_blocks = xs_hbm.shape[0] // EXPERT_BLOCK

    for t in range(tm):
        for k in range(TOP_K):
            _row_copy(h_ref, t, xs_hbm, dest_ref[0, 0, TOP_K * t + k], sem).start(priority=k % 2)

    @pl.when(i == 0)
    def _():
        for e in range(N_EXPERTS):
            def fill(r, c, e=e):
                _row_copy(zero_hbm, 0, xs_hbm, padbase_ref[e] + r, pad_sem).start()
                return c
            lax.fori_loop(0, padcnt_ref[e], fill, 0)

            def fill_sub(r, c, e=e):
                _rows_copy(zero_hbm, xs_hbm, subbase_ref[e] + r * SUB_BLOCK, SUB_BLOCK,
                           pad_sem).start()
                return c
            lax.fori_loop(0, subcnt_ref[e], fill_sub, 0)

        def fill_block(b, c):
            _rows_copy(zero_hbm, xs_hbm, b * EXPERT_BLOCK, EXPERT_BLOCK, pad_sem).start()
            return c
        lax.fori_loop(nreal_ref[0], n_blocks, fill_block, 0)

        for e in range(N_EXPERTS):
            def drain(r, c):
                _row_copy(zero_hbm, 0, xs_hbm, 0, pad_sem).wait()
                return c
            lax.fori_loop(0, padcnt_ref[e], drain, 0)

            def drain_sub(r, c):
                _rows_copy(zero_hbm, xs_hbm, 0, SUB_BLOCK, pad_sem).wait()
                return c
            lax.fori_loop(0, subcnt_ref[e], drain_sub, 0)

        def drain_block(b, c):
            _rows_copy(zero_hbm, xs_hbm, 0, EXPERT_BLOCK, pad_sem).wait()
            return c
        lax.fori_loop(nreal_ref[0], n_blocks, drain_block, 0)

    for _ in range(TOP_K * tm // DRAIN_ROWS):
        pltpu.make_async_copy(h_ref.at[pl.ds(0, DRAIN_ROWS)], xs_hbm.at[pl.ds(0, DRAIN_ROWS)],
                              sem).wait()


def _dispatch(fills, nreal, dest, h, n_rows):
    s = h.shape[0]
    tm = TM_DISP
    dest3 = dest.reshape(s // tm, 1, TOP_K * tm)
    zero_rows = jnp.zeros((EXPERT_BLOCK, D_MODEL), F32)
    grid_spec = pltpu.PrefetchScalarGridSpec(
        num_scalar_prefetch=5,
        grid=(s // tm,),
        in_specs=[
            pl.BlockSpec((1, 1, TOP_K * tm), lambda i, *_: (i, 0, 0), memory_space=pltpu.SMEM),
            pl.BlockSpec((tm, D_MODEL), lambda i, *_: (i, 0)),
            pl.BlockSpec(memory_space=pl.ANY),
        ],
        out_specs=pl.BlockSpec(memory_space=pl.ANY),
        scratch_shapes=[pltpu.SemaphoreType.DMA, pltpu.SemaphoreType.DMA],
    )
    return pl.pallas_call(
        _dispatch_kernel,
        grid_spec=grid_spec,
        out_shape=jax.ShapeDtypeStruct((n_rows, D_MODEL), F32),
        compiler_params=_cparams(("arbitrary",), 32),
        name="dispatch",
    )(*fills, nreal, dest3, h, zero_rows)


def _per_sub_block(nsub, o_ref, compute):
    @pl.when(nsub == SUBS_PER_BLOCK)
    def _():
        compute(slice(0, EXPERT_BLOCK))

    for n in range(SUBS_PER_BLOCK):
        @pl.when(nsub == n)
        def _(n=n):
            if n:
                compute(slice(0, n * SUB_BLOCK))
            o_ref[n * SUB_BLOCK:, :] = jnp.zeros((EXPERT_BLOCK - n * SUB_BLOCK, o_ref.shape[1]),
                                                 o_ref.dtype)


def _up_kernel(be_ref, nsub_ref, nreal_ref, nexte_ref, x_ref, w_hbm, bg_ref, bu_ref, perm_ref,
               a_ref, stage_ref, wg_ref, wu_ref, sem):
    half = pl.program_id(0)
    b = pl.program_id(1)
    e = be_ref[b]

    def fetch(expert, hf):
        cols = pl.ds(pl.multiple_of(hf * UP_HALF, UP_HALF), UP_HALF)
        return pltpu.make_async_copy(w_hbm.at[expert, :, cols], stage_ref, sem)

    @pl.when((half == 0) & (b == 0))
    def _():
        fetch(e, 0).start()

    @pl.when((b == 0) | (e != be_ref[jnp.maximum(b - 1, 0)]))
    def _():
        fetch(e, half).wait()
        for t in range(UP_HALF // SPLIT_TILE):
            w = stage_ref[:, t * SPLIT_TILE:(t + 1) * SPLIT_TILE].astype(BF16)
            r = jnp.dot(w, perm_ref[...], preferred_element_type=F32)
            cs = slice(t * (SPLIT_TILE // 2), (t + 1) * (SPLIT_TILE // 2))
            wg_ref[:, cs] = r[:, :SPLIT_TILE // 2].astype(BF16)
            wu_ref[:, cs] = r[:, SPLIT_TILE // 2:].astype(BF16)
        last_run = nexte_ref[b] == e

        @pl.when(jnp.logical_not(last_run))
        def _():
            fetch(nexte_ref[b], half).start()

        @pl.when(last_run & (half == 0))
        def _():
            fetch(be_ref[0], 1).start()

    def swiglu(rows):
        x = x_ref[rows, :].astype(BF16)
        for c in range(UP_HALF // 2 // UP_COLS):
            cs = slice(c * UP_COLS, (c + 1) * UP_COLS)
            g = jnp.dot(x, wg_ref[:, cs], preferred_element_type=F32) + bg_ref[0, :, cs]
            u = jnp.dot(x, wu_ref[:, cs], preferred_element_type=F32) + bu_ref[0, :, cs]
            g = jnp.minimum(g, SWIGLU_LIMIT)
            u = jnp.clip(u, -SWIGLU_LIMIT, SWIGLU_LIMIT)
            a_ref[rows, cs] = (g * jax.nn.sigmoid(SWIGLU_ALPHA * g) * (u + 1.0)).astype(a_ref.dtype)

    _per_sub_block(nsub_ref[b], a_ref, swiglu)


def _down_kernel(be_ref, nsub_ref, nreal_ref, nexte_ref, a_ref, w_hbm, b_ref, y_ref,
                 stage_ref, wbf_ref, sem):
    b = pl.program_id(0)
    e = be_ref[b]

    def fetch(expert):
        return pltpu.make_async_copy(w_hbm.at[expert], stage_ref, sem)

    @pl.when(b == 0)
    def _():
        fetch(e).start()

    @pl.when((b == 0) | (e != be_ref[jnp.maximum(b - 1, 0)]))
    def _():
        fetch(e).wait()
        wbf_ref[...] = stage_ref[...].astype(BF16)

        @pl.when(nexte_ref[b] != e)
        def _():
            fetch(nexte_ref[b]).start()

    def project(rows):
        y_ref[rows, :] = jnp.dot(a_ref[rows, :], wbf_ref[...], preferred_element_type=F32) + b_ref[0]

    _per_sub_block(nsub_ref[b], y_ref, project)


def _row_block(b, be, ns, nr, *_):
    return (jnp.minimum(b, nr[0] - 1), 0)


def _expert_block(b, be, *_):
    return (be[b], 0, 0)


def _experts(block_e, nsub, nreal, next_e, xs, w1, b1g, b1u, w2, b2):
    n_rows = xs.shape[0]
    nb = n_rows // EXPERT_BLOCK
    i = jnp.arange(SPLIT_TILE)[:, None]
    c = jnp.arange(SPLIT_TILE)[None, :]
    perm = jnp.where(c < SPLIT_TILE // 2, i == 2 * c, i == 2 * (c - SPLIT_TILE // 2) + 1).astype(BF16)
    act = pl.pallas_call(
        _up_kernel,
        grid_spec=pltpu.PrefetchScalarGridSpec(
            num_scalar_prefetch=4,
            grid=(2 * D_FF // UP_HALF, nb),
            in_specs=[
                pl.BlockSpec((EXPERT_BLOCK, D_MODEL),
                             lambda hf, b, be, ns, nr, ne: (jnp.minimum(b, nr[0] - 1), 0)),
                pl.BlockSpec(memory_space=pl.ANY),
                pl.BlockSpec((1, 1, UP_HALF // 2), lambda hf, b, be, *_: (be[b], 0, hf)),
                pl.BlockSpec((1, 1, UP_HALF // 2), lambda hf, b, be, *_: (be[b], 0, hf)),
                pl.BlockSpec((SPLIT_TILE, SPLIT_TILE), lambda hf, b, *_: (0, 0)),
            ],
            out_specs=pl.BlockSpec((EXPERT_BLOCK, UP_HALF // 2), lambda hf, b, *_: (b, hf)),
            scratch_shapes=[
                pltpu.VMEM((D_MODEL, UP_HALF), F32),
                pltpu.VMEM((D_MODEL, UP_HALF // 2), BF16),
                pltpu.VMEM((D_MODEL, UP_HALF // 2), BF16),
                pltpu.SemaphoreType.DMA,
            ],
        ),
        out_shape=jax.ShapeDtypeStruct((n_rows, D_FF), BF16),
        compiler_params=_cparams(("arbitrary", "arbitrary"), 56),
        name="expert_up",
    )(block_e, nsub, nreal, next_e, xs, w1, b1g, b1u, perm)
    return pl.pallas_call(
        _down_kernel,
        grid_spec=pltpu.PrefetchScalarGridSpec(
            num_scalar_prefetch=4,
            grid=(nb,),
            in_specs=[
                pl.BlockSpec((EXPERT_BLOCK, D_FF), _row_block),
                pl.BlockSpec(memory_space=pl.ANY),
                pl.BlockSpec((1, 1, D_MODEL), _expert_block),
            ],
            out_specs=pl.BlockSpec((EXPERT_BLOCK, D_MODEL), lambda b, *_: (b, 0)),
            scratch_shapes=[
                pltpu.VMEM((D_FF, D_MODEL), F32),
                pltpu.VMEM((D_FF, D_MODEL), BF16),
                pltpu.SemaphoreType.DMA,
            ],
        ),
        out_shape=jax.ShapeDtypeStruct((n_rows, D_MODEL), F32),
        compiler_params=_cparams(("arbitrary",), 56),
        name="expert_down",
    )(block_e, nsub, nreal, next_e, act, w2, b2)


def _combine_kernel(dcur_ref, dnxt_ref, x1_ref, meta_ref, g_ref, y_hbm, o_ref, ybuf, sem):
    i = pl.program_id(0)
    n = pl.num_programs(0)
    tm = TM_COMB
    slot = i % 2

    def row_gather(dref, s, t, k):
        return pltpu.make_async_copy(y_hbm.at[pl.ds(dref[0, 0, TOP_K * t + k], 1)],
                                     ybuf.at[s, k, pl.ds(t, 1)], sem.at[s])

    @pl.when(i == 0)
    def _():
        def body(t, c):
            for k in range(TOP_K):
                row_gather(dcur_ref, 0, t, k).start()
            return c
        lax.fori_loop(0, tm, body, 0, unroll=8)

    for s in range(2):
        @pl.when((i + 1 < n) & (slot == 1 - s))
        def _(s=s):
            for t in range(tm):
                for k in range(TOP_K):
                    row_gather(dnxt_ref, s, t, k).start(priority=k % 2)

    for k in range(TOP_K):
        pltpu.make_async_copy(y_hbm.at[pl.ds(0, tm)], ybuf.at[slot, k], sem.at[slot]).wait()

    meta = meta_ref[...]
    acc = x1_ref[...]
    for k in range(TOP_K):
        acc = acc + meta[:, META_W + k:META_W + k + 1] * ybuf[slot, k]
    var = jnp.mean(acc * acc, axis=-1, keepdims=True)
    o_ref[...] = (acc * lax.rsqrt(var + RMS_EPS) * g_ref[...]).astype(o_ref.dtype)


def _combine(dest, x1, meta, g, y):
    s = x1.shape[0]
    tm = TM_COMB
    nt = s // tm
    dest3 = dest.reshape(nt, 1, TOP_K * tm)
    return pl.pallas_call(
        _combine_kernel,
        grid=(nt,),
        in_specs=[
            pl.BlockSpec((1, 1, TOP_K * tm), lambda i: (i, 0, 0), memory_space=pltpu.SMEM),
            pl.BlockSpec((1, 1, TOP_K * tm), lambda i: (jnp.minimum(i + 1, nt - 1), 0, 0),
                         memory_space=pltpu.SMEM),
            pl.BlockSpec((tm, D_MODEL), lambda i: (i, 0)),
            pl.BlockSpec((tm, LANES), lambda i: (i, 0)),
            pl.BlockSpec((1, D_MODEL), lambda i: (0, 0)),
            pl.BlockSpec(memory_space=pl.ANY),
        ],
        out_specs=pl.BlockSpec((tm, D_MODEL), lambda i: (i, 0)),
        out_shape=jax.ShapeDtypeStruct((s, D_MODEL), F32),
        scratch_shapes=[
            pltpu.VMEM((2, TOP_K, tm, D_MODEL), F32),
            pltpu.SemaphoreType.DMA((2,)),
        ],
        compiler_params=_cparams(("arbitrary",), 40),
        name="combine",
    )(dest3, dest3, x1, meta, g, y)


def _routing_tables(meta, cnt, n_blocks):
    counts = cnt[0].astype(jnp.int32)
    padded = ((counts + EXPERT_BLOCK - 1) // EXPERT_BLOCK) * EXPERT_BLOCK
    pad_end = jnp.cumsum(padded)
    pad_start = pad_end - padded
    e4 = meta[:, META_E:META_E + TOP_K].astype(jnp.int32)
    pos4 = meta[:, META_POS:META_POS + TOP_K].astype(jnp.int32)
    experts = jnp.arange(N_EXPERTS, dtype=jnp.int32)
    dest = pos4 + jnp.sum(jnp.where(e4[..., None] == experts, pad_start, 0), axis=-1)
    nreal = pad_end[-1:] // EXPERT_BLOCK
    blk_raw = jnp.arange(n_blocks, dtype=jnp.int32)
    blk = jnp.minimum(blk_raw, nreal[0] - 1)
    block_e = jnp.sum(pad_end[None, :] <= (blk * EXPERT_BLOCK)[:, None], axis=1).astype(jnp.int32)
    onehot_e = block_e[:, None] == experts
    pick = lambda table: jnp.sum(jnp.where(onehot_e, table, 0), axis=1)
    rows_left = pick(counts) - (blk * EXPERT_BLOCK - pick(pad_start))
    nsub = jnp.clip((rows_left + SUB_BLOCK - 1) // SUB_BLOCK, 0, SUBS_PER_BLOCK)
    nsub = jnp.where(blk_raw < nreal[0], nsub, 0)
    later = experts[None, :] > block_e[:, None]
    next_e = jnp.min(jnp.where(later & (counts > 0)[None, :], experts, N_EXPERTS), axis=1)
    next_e = jnp.where(next_e == N_EXPERTS, block_e, next_e)
    sub_rows = ((counts + SUB_BLOCK - 1) // SUB_BLOCK) * SUB_BLOCK
    fills = (pad_start + counts, sub_rows - counts, pad_start + sub_rows,
             (padded - sub_rows) // SUB_BLOCK)
    i32 = lambda v: v.astype(jnp.int32)
    return dest, i32(block_e), i32(nsub), i32(nreal), i32(next_e), tuple(i32(f) for f in fills)


def kernel(x, norm_mix, w_in, b_gate, rel_bias, w_attn_proj, pool_w, pool_scale, w_pool_proj,
           w_out, norm_ffn, router_w, router_b, w1, b1, w2, b2, norm_final):
    bsz, seq, d = x.shape
    n_tok = bsz * seq
    assert w_in.shape[0] == 1 and bsz == 1 and d == D_MODEL and seq % TM_IN == 0
    n_blocks = -(-(n_tok * TOP_K) // EXPERT_BLOCK) + N_EXPERTS
    n_rows = n_blocks * EXPERT_BLOCK
    x2 = x.reshape(n_tok, d)
    row = lambda v: v.reshape(1, -1).astype(F32)
    l = 0
    proj = _inproj(x2, row(norm_mix[l]), w_in[l].astype(BF16))
    attn = _attention(proj, _attn_bias_table(rel_bias[l]))
    merged = _mix(attn, proj, row(b_gate[l]), w_attn_proj[l].astype(BF16),
                  w_pool_proj[l].astype(BF16), pool_w[l].astype(BF16), row(pool_scale[l]))
    rw = router_w[l].astype(F32)
    rw_hi = rw.astype(BF16)
    rw_lo = (rw - rw_hi.astype(F32)).astype(BF16)
    x1, h, meta, cnt = _post(x2, merged, w_out[l].astype(BF16), row(norm_ffn[l]),
                             rw_hi, rw_lo, row(router_b[l]))
    dest, block_e, nsub, nreal, next_e, fills = _routing_tables(meta, cnt, n_blocks)
    xs = _dispatch(fills, nreal, dest, h, n_rows)
    y = _experts(block_e, nsub, nreal, next_e, xs, w1[l],
                 b1[l][:, None, 0::2].astype(F32), b1[l][:, None, 1::2].astype(F32),
                 w2[l], b2[l][:, None, :].astype(F32))
    out = _combine(dest, x1, meta, row(norm_final), y)
    return out.reshape(bsz, seq, d)
```
